```python
import jax, jax.numpy as jnp
from jax import lax
import numpy as np

D_MODEL = 2048
BATCH = 8
SEQ = 2048
DEPTH = 2

RET_HEADS = 4
RET_QK_DIM = 256
RET_V_DIM = 512
RET_CHUNK = 128
D_QK = RET_HEADS * RET_QK_DIM
D_RV = RET_HEADS * RET_V_DIM
ROPE_BASE = 10000.0
D_CONV = D_MODEL
CONV_K = 3
D_FF = 5632
N_EXPERTS = 8
TOP_K = 2
D_FF_EXPERT = 5632
EPS = 1e-6
N_DENSE = (DEPTH + 1) // 2
N_MOE = DEPTH // 2
IN_SIZES = [D_QK, D_QK, D_RV, D_RV, D_CONV, D_CONV, D_CONV, D_MODEL, D_MODEL]
D_IN_TOTAL = int(sum(IN_SIZES))
SPLIT_POINTS = [int(s) for s in np.cumsum(IN_SIZES)[:-1]]

kernel_name = "hybrid_retention_shortconv_gated_moe"


def rmsnorm(x, g):
    xf = x.astype(jnp.float32)
    y = xf * lax.rsqrt(jnp.mean(xf * xf, axis=-1, keepdims=True) + EPS)
    return (y * g.astype(jnp.float32)).astype(x.dtype)


def rotary(x, pos):
    half = x.shape[-1] // 2
    inv_freq = ROPE_BASE ** (-jnp.arange(half, dtype=jnp.float32) / half)
    ang = pos.astype(jnp.float32)[..., None] * inv_freq
    cos = jnp.cos(ang)[:, :, None, :]
    sin = jnp.sin(ang)[:, :, None, :]
    xf = x.astype(jnp.float32)
    x1, x2 = xf[..., :half], xf[..., half:]
    return jnp.concatenate([x1 * cos - x2 * sin, x1 * sin + x2 * cos], axis=-1)


def retention(q, k, v):
    bsz, t, h, dk = q.shape
    dv = v.shape[-1]
    n_chunks = t // RET_CHUNK
    log_gamma = jnp.log1p(-jnp.power(2.0, -5.0 - jnp.arange(h, dtype=jnp.float32)))
    idx = jnp.arange(RET_CHUNK, dtype=jnp.float32)
    diff = idx[:, None] - idx[None, :]
    causal = diff >= 0
    decay_mask = jnp.where(causal[None],
                           jnp.exp(jnp.where(causal, diff, 0.0)[None] * log_gamma[:, None, None]),
                           0.0)
    xi = jnp.exp((idx + 1.0)[None] * log_gamma[:, None])
    zeta = jnp.exp((RET_CHUNK - 1.0 - idx)[None] * log_gamma[:, None])
    gamma_chunk = jnp.exp(RET_CHUNK * log_gamma)

    def to_chunks(a):
        return a.reshape(bsz, n_chunks, RET_CHUNK, h, a.shape[-1]).transpose(1, 0, 3, 2, 4)

    def step(state, qkv):
        qc, kc, vc = qkv
        scores = jnp.einsum('bhqd,bhkd->bhqk', qc, kc) * decay_mask[None]
        inner = jnp.einsum('bhqk,bhkv->bhqv', scores, vc)
        cross = jnp.einsum('bhqd,bhdv->bhqv', qc, state) * xi[None, :, :, None]
        new_state = state * gamma_chunk[None, :, None, None] + jnp.einsum(
            'bhkd,bhkv->bhdv', kc * zeta[None, :, :, None], vc)
        return new_state, inner + cross

    state0 = jnp.zeros((bsz, h, dk, dv), jnp.float32)
    _, out = lax.scan(step, state0, (to_chunks(q), to_chunks(k), to_chunks(v)))
    return out.transpose(1, 0, 3, 2, 4).reshape(bsz, t, h, dv)


def head_groupnorm(o):
    mu = jnp.mean(o, axis=-1, keepdims=True)
    var = jnp.mean(jnp.square(o - mu), axis=-1, keepdims=True)
    return (o - mu) * lax.rsqrt(var + EPS)


def causal_depthwise_conv(u, w):
    c = u.shape[-1]
    return lax.conv_general_dilated(
        u, w[:, None, :].astype(u.dtype), window_strides=(1,),
        padding=[(CONV_K - 1, 0)], dimension_numbers=('NWC', 'WIO', 'NWC'),
        feature_group_count=c)


def swiglu(x, w1, w3, w2):
    return (jax.nn.silu(x @ w1) * (x @ w3)) @ w2


def moe_swiglu(x, router, w1, w3, w2):
    xt = x.reshape(-1, x.shape[-1])
    logits = (xt @ router).astype(jnp.float32)
    top_val, top_idx = lax.top_k(logits, TOP_K)
    gates = jax.nn.softmax(top_val, axis=-1)
    out = jnp.zeros(xt.shape, jnp.float32)
    for e in range(N_EXPERTS):
        w_e = jnp.sum(jnp.where(top_idx == e, gates, 0.0), axis=-1)
        out = out + w_e[:, None] * swiglu(xt, w1[e], w3[e], w2[e]).astype(jnp.float32)
    return out.astype(x.dtype).reshape(x.shape)


def setup_inputs(seed: int = 0) -> dict:
    key = jax.random.key(seed)
    ks = jax.random.split(key, 18)
    f32 = jnp.float32

    def nrm(k, shape, fan_in):
        return jax.random.normal(k, shape, f32) * (fan_in ** -0.5)

    x = jax.random.normal(ks[0], (BATCH, SEQ, D_MODEL), f32)
    offsets = jax.random.randint(ks[1], (BATCH, 1), 0, 1024, dtype=jnp.int32)
    positions = (offsets + jnp.arange(SEQ, dtype=jnp.int32)[None, :]).astype(jnp.int32)
    norm_mix = 1.0 + 0.02 * jax.random.normal(ks[2], (DEPTH, D_MODEL), f32)
    norm_ffn = 1.0 + 0.02 * jax.random.normal(ks[3], (DEPTH, D_MODEL), f32)
    norm_final = 1.0 + 0.02 * jax.random.normal(ks[4], (D_MODEL,), f32)
    w_in = nrm(ks[5], (DEPTH, D_MODEL, D_IN_TOTAL), D_MODEL)
    conv_w = nrm(ks[6], (DEPTH, CONV_K, D_CONV), CONV_K)
    w_ret_out = nrm(ks[7], (DEPTH, D_RV, D_MODEL), D_RV)
    w_conv_out = nrm(ks[8], (DEPTH, D_CONV, D_MODEL), D_CONV)
    w_out = nrm(ks[9], (DEPTH, D_MODEL, D_MODEL), D_MODEL)
    ffn_w1 = nrm(ks[10], (N_DENSE, D_MODEL, D_FF), D_MODEL)
    ffn_w3 = nrm(ks[11], (N_DENSE, D_MODEL, D_FF), D_MODEL)
    ffn_w2 = nrm(ks[12], (N_DENSE, D_FF, D_MODEL), D_FF)
    router = nrm(ks[13], (N_MOE, D_MODEL, N_EXPERTS), D_MODEL)
    moe_w1 = nrm(ks[14], (N_MOE, N_EXPERTS, D_MODEL, D_FF_EXPERT), D_MODEL)
    moe_w3 = nrm(ks[15], (N_MOE, N_EXPERTS, D_MODEL, D_FF_EXPERT), D_MODEL)
    moe_w2 = nrm(ks[16], (N_MOE, N_EXPERTS, D_FF_EXPERT, D_MODEL), D_FF_EXPERT)
    return {"x": x, "positions": positions, "norm_mix": norm_mix, "norm_ffn": norm_ffn,
            "norm_final": norm_final, "w_in": w_in, "conv_w": conv_w, "w_ret_out": w_ret_out,
            "w_conv_out": w_conv_out, "w_out": w_out, "ffn_w1": ffn_w1, "ffn_w3": ffn_w3,
            "ffn_w2": ffn_w2, "router": router, "moe_w1": moe_w1, "moe_w3": moe_w3,
            "moe_w2": moe_w2}


def reference(x, positions, norm_mix, norm_ffn, norm_final, w_in, conv_w, w_ret_out,
              w_conv_out, w_out, ffn_w1, ffn_w3, ffn_w2, router, moe_w1, moe_w3, moe_w2):
    bsz, t, _ = x.shape
    h = x
    for layer in range(DEPTH):
        xn = rmsnorm(h, norm_mix[layer])
        proj = xn @ w_in[layer]
        q, k, v, g, cb, cc, ch, ga, gb = jnp.split(proj, SPLIT_POINTS, axis=-1)
        q = rotary(q.reshape(bsz, t, RET_HEADS, RET_QK_DIM), positions) * (RET_QK_DIM ** -0.5)
        k = rotary(k.reshape(bsz, t, RET_HEADS, RET_QK_DIM), positions)
        v = v.reshape(bsz, t, RET_HEADS, RET_V_DIM).astype(jnp.float32)
        o = head_groupnorm(retention(q, k, v)).reshape(bsz, t, D_RV).astype(h.dtype)
        y_ret = (o * jax.nn.silu(g)) @ w_ret_out[layer]
        y_conv = (cb * causal_depthwise_conv(cc * ch, conv_w[layer])) @ w_conv_out[layer]
        merged = jax.nn.sigmoid(ga) * y_ret + jax.nn.sigmoid(gb) * y_conv
        h = h + merged @ w_out[layer]
        xn = rmsnorm(h, norm_ffn[layer])
        if layer % 2 == 0:
            i = layer // 2
            h = h + swiglu(xn, ffn_w1[i], ffn_w3[i], ffn_w2[i])
        else:
            i = layer // 2
            h = h + moe_swiglu(xn, router[i], moe_w1[i], moe_w3[i], moe_w2[i])
    return rmsnorm(h, norm_final)
```

```python
import functools

import jax
import jax.numpy as jnp
from jax import lax
from jax.experimental import pallas as pl
from jax.experimental.pallas import tpu as pltpu

F32 = jnp.float32
BF16 = jnp.bfloat16

D_MODEL = 2048
RET_HEADS = 4
RET_QK_DIM = 256
RET_V_DIM = 512
RET_CHUNK = 128
ROPE_BASE = 10000.0
CONV_K = 3
N_EXPERTS = 8
TOP_K = 2
EPS = 1e-6
HALF = RET_QK_DIM // 2

OFF_Q, OFF_K, OFF_V, OFF_G = 0, 1024, 2048, 4096
OFF_CB, OFF_CC, OFF_CH, OFF_GA, OFF_GB = 6144, 8192, 10240, 12288, 14336
D_IN_TOTAL = 16384

LANES = 128
VMEM_LIMIT = 56 * 1024 * 1024

ROW_TILE = 1024
SUB_ROWS = 256
FFN_COLS = 256
ROUTE_TILE = 512
SCATTER_TILE = 512
COMBINE_TILE = 256


def _params(sem, vmem=VMEM_LIMIT):
    return pltpu.CompilerParams(dimension_semantics=sem, vmem_limit_bytes=vmem)


def _rms_rows(x, g):
    ms = jnp.mean(x * x, axis=-1, keepdims=True)
    return x * lax.rsqrt(ms + EPS) * g


def _sigmoid(x):
    return 1.0 / (1.0 + jnp.exp(-x))


def _rope_kernel(pos_ref, invf_ref, cos_ref, sin_ref):
    ang = pos_ref[...].astype(F32) * invf_ref[...]
    cos_ref[...] = jnp.cos(ang)
    sin_ref[...] = jnp.sin(ang)


def _rope_tables(positions):
    n = positions.size
    inv_freq = ROPE_BASE ** (-jnp.arange(HALF, dtype=F32) / HALF)
    rows = 2048
    return pl.pallas_call(
        _rope_kernel,
        grid=(n // rows,),
        in_specs=[pl.BlockSpec((rows, 1), lambda i: (i, 0)),
                  pl.BlockSpec((1, HALF), lambda i: (0, 0))],
        out_specs=[pl.BlockSpec((rows, HALF), lambda i: (i, 0)),
                   pl.BlockSpec((rows, HALF), lambda i: (i, 0))],
        out_shape=[jax.ShapeDtypeStruct((n, HALF), F32)] * 2,
        compiler_params=_params(("parallel",)),
        name="rope_tables",
    )(positions.reshape(n, 1), inv_freq.reshape(1, HALF))


def _inproj_kernel(h_ref, g_ref, w_ref, o_ref, xn_ref):
    @pl.when(pl.program_id(1) == 0)
    def _():
        def body(r, c):
            sl = pl.ds(pl.multiple_of(r * SUB_ROWS, SUB_ROWS), SUB_ROWS)
            xn_ref[sl, :] = _rms_rows(h_ref[sl, :], g_ref[...]).astype(BF16)
            return c
        lax.fori_loop(0, h_ref.shape[0] // SUB_ROWS, body, 0)

    o_ref[...] = jnp.dot(xn_ref[...], w_ref[...].astype(BF16),
                         preferred_element_type=F32).astype(o_ref.dtype)


def _inproj(h, g, w_in, layer):
    n = h.shape[0]
    tn = 512
    return pl.pallas_call(
        _inproj_kernel,
        grid=(n // ROW_TILE, D_IN_TOTAL // tn),
        in_specs=[pl.BlockSpec((ROW_TILE, D_MODEL), lambda i, j: (i, 0)),
                  pl.BlockSpec((None, 1, D_MODEL), lambda i, j: (layer, 0, 0)),
                  pl.BlockSpec((None, D_MODEL, tn), lambda i, j: (layer, 0, j))],
        out_specs=pl.BlockSpec((ROW_TILE, tn), lambda i, j: (i, j)),
        out_shape=jax.ShapeDtypeStruct((n, D_IN_TOTAL), BF16),
        scratch_shapes=[pltpu.VMEM((ROW_TILE, D_MODEL), BF16)],
        compiler_params=_params(("parallel", "arbitrary")),
        name="inproj",
    )(h, g, w_in)


def _retention_kernel(gc_ref, q_ref, k_ref, v_ref, g_ref, cos_ref, sin_ref, dm_ref, xi_ref,
                      zeta_ref, o_ref, qr_ref, kr_ref, st_ref):
    seq = q_ref.shape[0]
    scale = RET_QK_DIM ** -0.5

    def rot(r, c):
        sl = pl.ds(pl.multiple_of(r * SUB_ROWS, SUB_ROWS), SUB_ROWS)
        cos = cos_ref[sl, :]
        sin = sin_ref[sl, :]
        q = q_ref[sl, :].astype(F32)
        q1, q2 = q[:, :HALF], q[:, HALF:]
        qr_ref[sl, :HALF] = ((q1 * cos - q2 * sin) * scale).astype(BF16)
        qr_ref[sl, HALF:] = ((q1 * sin + q2 * cos) * scale).astype(BF16)
        k = k_ref[sl, :].astype(F32)
        k1, k2 = k[:, :HALF], k[:, HALF:]
        kr_ref[sl, :HALF] = (k1 * cos - k2 * sin).astype(BF16)
        kr_ref[sl, HALF:] = (k1 * sin + k2 * cos).astype(BF16)
        return c
    lax.fori_loop(0, seq // SUB_ROWS, rot, 0)

    st_ref[...] = jnp.zeros_like(st_ref)
    gamma_chunk = gc_ref[pl.program_id(1)]

    def chunk(c, carry):
        sl = pl.ds(pl.multiple_of(c * RET_CHUNK, RET_CHUNK), RET_CHUNK)
        qc = qr_ref[sl, :]
        kc = kr_ref[sl, :]
        vc = v_ref[sl, :]
        scores = lax.dot_general(qc, kc, (((1,), (1,)), ((), ())),
                                 preferred_element_type=F32) * dm_ref[...]
        inner = jnp.dot(scores.astype(BF16), vc, preferred_element_type=F32)
        state = st_ref[...]
        cross = jnp.dot(qc, state.astype(BF16), preferred_element_type=F32) * xi_ref[...]
        kz = (kc.astype(F32) * zeta_ref[...]).astype(BF16)
        st_ref[...] = state * gamma_chunk + lax.dot_general(
            kz, vc, (((0,), (0,)), ((), ())), preferred_element_type=F32)
        o = inner + cross
        mu = jnp.mean(o, axis=-1, keepdims=True)
        d = o - mu
        var = jnp.mean(d * d, axis=-1, keepdims=True)
        on = d * lax.rsqrt(var + EPS)
        g = g_ref[sl, :].astype(F32)
        o_ref[sl, :] = (on * (g * _sigmoid(g))).astype(o_ref.dtype)
        return carry
    lax.fori_loop(0, seq // RET_CHUNK, chunk, 0)


def _retention_consts():
    h = RET_HEADS
    log_gamma = jnp.log1p(-jnp.power(2.0, -5.0 - jnp.arange(h, dtype=F32)))
    idx = jnp.arange(RET_CHUNK, dtype=F32)
    diff = idx[:, None] - idx[None, :]
    causal = diff >= 0
    decay_mask = jnp.where(causal[None],
                           jnp.exp(jnp.where(causal, diff, 0.0)[None] * log_gamma[:, None, None]),
                           0.0)
    xi = jnp.exp((idx + 1.0)[None] * log_gamma[:, None])
    zeta = jnp.exp((RET_CHUNK - 1.0 - idx)[None] * log_gamma[:, None])
    gamma_chunk = jnp.exp(RET_CHUNK * log_gamma)
    xi_b = jnp.broadcast_to(xi[:, :, None], (h, RET_CHUNK, RET_V_DIM))
    zeta_b = jnp.broadcast_to(zeta[:, :, None], (h, RET_CHUNK, RET_QK_DIM))
    return decay_mask, xi_b, zeta_b, gamma_chunk


def _retention(proj, cos, sin, consts, bsz, seq):
    decay_mask, xi_b, zeta_b, gamma_chunk = consts
    n = proj.shape[0]
    qb, vb = RET_QK_DIM, RET_V_DIM
    return pl.pallas_call(
        _retention_kernel,
        grid=(bsz, RET_HEADS),
        in_specs=[
            pl.BlockSpec(memory_space=pltpu.SMEM),
            pl.BlockSpec((seq, qb), lambda b, h: (b, OFF_Q // qb + h)),
            pl.BlockSpec((seq, qb), lambda b, h: (b, OFF_K // qb + h)),
            pl.BlockSpec((seq, vb), lambda b, h: (b, OFF_V // vb + h)),
            pl.BlockSpec((seq, vb), lambda b, h: (b, OFF_G // vb + h)),
            pl.BlockSpec((seq, HALF), lambda b, h: (b, 0)),
            pl.BlockSpec((seq, HALF), lambda b, h: (b, 0)),
            pl.BlockSpec((None, RET_CHUNK, RET_CHUNK), lambda b, h: (h, 0, 0)),
            pl.BlockSpec((None, RET_CHUNK, vb), lambda b, h: (h, 0, 0)),
            pl.BlockSpec((None, RET_CHUNK, qb), lambda b, h: (h, 0, 0)),
        ],
        out_specs=pl.BlockSpec((seq, vb), lambda b, h: (b, h)),
        scratch_shapes=[pltpu.VMEM((seq, qb), BF16), pltpu.VMEM((seq, qb), BF16),
                        pltpu.VMEM((qb, vb), F32)],
        out_shape=jax.ShapeDtypeStruct((n, RET_HEADS * vb), BF16),
        compiler_params=_params(("parallel", "parallel")),
        name="retention",
    )(gamma_chunk, proj, proj, proj, proj, cos, sin, decay_mask, xi_b, zeta_b)


def _conv_kernel(cb_ref, cc_ref, ch_ref, w_ref, o_ref):
    u = cc_ref[...].astype(F32) * ch_ref[...].astype(F32)
    row = lax.broadcasted_iota(jnp.int32, u.shape, 0)
    u1 = jnp.where(row >= 1, pltpu.roll(u, 1, 0), 0.0)
    u2 = jnp.where(row >= 2, pltpu.roll(u, 2, 0), 0.0)
    w = w_ref[...]
    y = w[0:1, :] * u2 + w[1:2, :] * u1 + w[2:3, :] * u
    o_ref[...] = (cb_ref[...].astype(F32) * y).astype(o_ref.dtype)


def _short_conv(proj, conv_w, layer, bsz, seq):
    n = proj.shape[0]
    cw = 256
    return pl.pallas_call(
        _conv_kernel,
        grid=(bsz, D_MODEL // cw),
        in_specs=[pl.BlockSpec((seq, cw), lambda b, j: (b, OFF_CB // cw + j)),
                  pl.BlockSpec((seq, cw), lambda b, j: (b, OFF_CC // cw + j)),
                  pl.BlockSpec((seq, cw), lambda b, j: (b, OFF_CH // cw + j)),
                  pl.BlockSpec((None, CONV_K, cw), lambda b, j: (layer, 0, j))],
        out_specs=pl.BlockSpec((seq, cw), lambda b, j: (b, j)),
        out_shape=jax.ShapeDtypeStruct((n, D_MODEL), BF16),
        compiler_params=_params(("parallel", "parallel")),
        name="short_conv",
    )(proj, proj, proj, conv_w)


def _merge_kernel(og_ref, cv_ref, wr_ref, wc_ref, ga_ref, gb_ref, o_ref):
    yr = jnp.dot(og_ref[...], wr_ref[...].astype(BF16), preferred_element_type=F32)
    yc = jnp.dot(cv_ref[...], wc_ref[...].astype(BF16), preferred_element_type=F32)
    ga = ga_ref[...].astype(F32)
    gb = gb_ref[...].astype(F32)
    o_ref[...] = (_sigmoid(ga) * yr + _sigmoid(gb) * yc).astype(o_ref.dtype)


def _merge(og, cv, proj, w_ret_out, w_conv_out, layer):
    n = og.shape[0]
    tn = 512
    return pl.pallas_call(
        _merge_kernel,
        grid=(n // ROW_TILE, D_MODEL // tn),
        in_specs=[pl.BlockSpec((ROW_TILE, D_MODEL), lambda i, j: (i, 0)),
                  pl.BlockSpec((ROW_TILE, D_MODEL), lambda i, j: (i, 0)),
                  pl.BlockSpec((None, D_MODEL, tn), lambda i, j: (layer, 0, j)),
                  pl.BlockSpec((None, D_MODEL, tn), lambda i, j: (layer, 0, j)),
                  pl.BlockSpec((ROW_TILE, tn), lambda i, j: (i, OFF_GA // tn + j)),
                  pl.BlockSpec((ROW_TILE, tn), lambda i, j: (i, OFF_GB // tn + j))],
        out_specs=pl.BlockSpec((ROW_TILE, tn), lambda i, j: (i, j)),
        out_shape=jax.ShapeDtypeStruct((n, D_MODEL), BF16),
        compiler_params=_params(("parallel", "arbitrary")),
        name="merge",
    )(og, cv, w_ret_out, w_conv_out, proj, proj)


def _outproj_kernel(m_ref, w_ref, h_ref, o_ref):
    o_ref[...] = h_ref[...] + jnp.dot(m_ref[...], w_ref[...].astype(BF16),
                                      preferred_element_type=F32)


def _outproj(merged, w_out, h, layer):
    n = h.shape[0]
    tn = 512
    return pl.pallas_call(
        _outproj_kernel,
        grid=(n // ROW_TILE, D_MODEL // tn),
        in_specs=[pl.BlockSpec((ROW_TILE, D_MODEL), lambda i, j: (i, 0)),
                  pl.BlockSpec((None, D_MODEL, tn), lambda i, j: (layer, 0, j)),
                  pl.BlockSpec((ROW_TILE, tn), lambda i, j: (i, j))],
        out_specs=pl.BlockSpec((ROW_TILE, tn), lambda i, j: (i, j)),
        out_shape=jax.ShapeDtypeStruct((n, D_MODEL), F32),
        compiler_params=_params(("parallel", "arbitrary")),
        name="outproj",
    )(merged, w_out, h)


def _ffn_kernel(te_ref, tv_ref, x_ref, g_ref, w1_ref, w3_ref, w2_ref, o_ref,
                xb_ref, w1b_ref, w3b_ref, w2b_ref, *, fused_norm_residual):
    i = pl.program_id(0)
    f = pl.program_id(1)
    n_sub = x_ref.shape[0] // SUB_ROWS
    valid = tv_ref[i] > 0

    @pl.when(f == 0)
    def _():
        def body(r, c):
            sl = pl.ds(pl.multiple_of(r * SUB_ROWS, SUB_ROWS), SUB_ROWS)
            x = x_ref[sl, :]
            if fused_norm_residual:
                xb_ref[sl, :] = _rms_rows(x, g_ref[...]).astype(BF16)
                o_ref[sl, :] = x
            else:
                xb_ref[sl, :] = x.astype(BF16)
                o_ref[sl, :] = jnp.zeros_like(x)
            return c
        lax.fori_loop(0, n_sub, body, 0)

    @pl.when(valid)
    def _():
        w1b_ref[...] = w1_ref[...].astype(BF16)
        w3b_ref[...] = w3_ref[...].astype(BF16)
        w2b_ref[...] = w2_ref[...].astype(BF16)

        def body(r, c):
            sl = pl.ds(pl.multiple_of(r * SUB_ROWS, SUB_ROWS), SUB_ROWS)
            x = xb_ref[sl, :]
            a = jnp.dot(x, w1b_ref[...], preferred_element_type=F32)
            b = jnp.dot(x, w3b_ref[...], preferred_element_type=F32)
            hh = (a * _sigmoid(a) * b).astype(BF16)
            o_ref[sl, :] += jnp.dot(hh, w2b_ref[...], preferred_element_type=F32)
            return c
        lax.fori_loop(0, n_sub, body, 0)


def _ffn(x, g, g_row, w1, w3, w2, w_idx, tile_expert, tile_valid, fused_norm_residual):
    rows = x.shape[0]
    d_ff = w1.shape[-1]
    tf = FFN_COLS
    nf = d_ff // tf
    n_tiles = rows // ROW_TILE

    def wcol(i, f, te, tv):
        return jnp.where(tv[i] > 0, f, nf - 1)

    kern = functools.partial(_ffn_kernel, fused_norm_residual=fused_norm_residual)
    return pl.pallas_call(
        kern,
        grid_spec=pltpu.PrefetchScalarGridSpec(
            num_scalar_prefetch=2,
            grid=(n_tiles, nf),
            in_specs=[
                pl.BlockSpec((ROW_TILE, D_MODEL), lambda i, f, te, tv: (i, 0),
                             pipeline_mode=pl.Buffered(1)),
                pl.BlockSpec((None, 1, D_MODEL), lambda i, f, te, tv: (g_row, 0, 0)),
                pl.BlockSpec((None, None, D_MODEL, tf),
                             lambda i, f, te, tv: (w_idx, te[i], 0, wcol(i, f, te, tv))),
                pl.BlockSpec((None, None, D_MODEL, tf),
                             lambda i, f, te, tv: (w_idx, te[i], 0, wcol(i, f, te, tv))),
                pl.BlockSpec((None, None, tf, D_MODEL),
                             lambda i, f, te, tv: (w_idx, te[i], wcol(i, f, te, tv), 0)),
            ],
            out_specs=pl.BlockSpec((ROW_TILE, D_MODEL), lambda i, f, te, tv: (i, 0)),
            scratch_shapes=[pltpu.VMEM((ROW_TILE, D_MODEL), BF16),
                            pltpu.VMEM((D_MODEL, tf), BF16),
                            pltpu.VMEM((D_MODEL, tf), BF16),
                            pltpu.VMEM((tf, D_MODEL), BF16)],
        ),
        out_shape=jax.ShapeDtypeStruct((rows, D_MODEL), F32),
        compiler_params=_params(("parallel", "arbitrary")),
        name="ffn_dense" if fused_norm_residual else "ffn_experts",
    )(tile_expert, tile_valid, x, g, w1, w3, w2)


def _route_kernel(h_ref, g_ref, rh_ref, rl_ref, xn_ref, info_ref, cnt_ref, run_ref):
    t = h_ref.shape[0]

    @pl.when(pl.program_id(0) == 0)
    def _():
        run_ref[...] = jnp.zeros_like(run_ref)

    xn = _rms_rows(h_ref[...], g_ref[...])
    xn_ref[...] = xn
    xh = xn.astype(BF16)
    xl = (xn - xh.astype(F32)).astype(BF16)
    logits = (jnp.dot(xh, rh_ref[...], preferred_element_type=F32)
              + jnp.dot(xl, rh_ref[...], preferred_element_type=F32)
              + jnp.dot(xh, rl_ref[...], preferred_element_type=F32))
    lane = lax.broadcasted_iota(jnp.int32, (t, LANES), 1).astype(F32)
    neg = jnp.float32(-jnp.inf)
    logits = jnp.where(lane < N_EXPERTS, logits, neg)
    v1 = jnp.max(logits, axis=-1, keepdims=True)
    e1 = jnp.min(jnp.where(logits == v1, lane, float(LANES)), axis=-1, keepdims=True)
    rest = jnp.where(lane == e1, neg, logits)
    v2 = jnp.max(rest, axis=-1, keepdims=True)
    e2 = jnp.min(jnp.where(rest == v2, lane, float(LANES)), axis=-1, keepdims=True)
    ex = jnp.exp(v2 - v1)
    g1 = 1.0 / (1.0 + ex)
    g2 = ex / (1.0 + ex)

    onehot = jnp.where(jnp.logical_or(lane == e1, lane == e2), 1.0, 0.0)
    r_i = lax.broadcasted_iota(jnp.int32, (t, t), 0)
    c_i = lax.broadcasted_iota(jnp.int32, (t, t), 1)
    tri = jnp.where(c_i <= r_i, 1.0, 0.0).astype(BF16)
    cum = jnp.dot(tri, onehot.astype(BF16), preferred_element_type=F32)
    excl = cum - onehot + run_ref[...]
    rank1 = jnp.sum(jnp.where(lane == e1, excl, 0.0), axis=-1, keepdims=True)
    rank2 = jnp.sum(jnp.where(lane == e2, excl, 0.0), axis=-1, keepdims=True)
    run_ref[...] = run_ref[...] + cum[t - 1:t, :]

    info = jnp.where(lane == 0, e1, 0.0)
    info = jnp.where(lane == 1, e2, info)
    info = jnp.where(lane == 2, g1, info)
    info = jnp.where(lane == 3, g2, info)
    info = jnp.where(lane == 4, rank1, info)
    info = jnp.where(lane == 5, rank2, info)
    info_ref[...] = info
    cnt_ref[...] = jnp.broadcast_to(run_ref[...], cnt_ref.shape)


def _route(h, g, router, layer):
    n = h.shape[0]
    r = jnp.zeros((D_MODEL, LANES), F32).at[:, :N_EXPERTS].set(router)
    rh = r.astype(BF16)
    rl = (r - rh.astype(F32)).astype(BF16)
    t = ROUTE_TILE
    return pl.pallas_call(
        _route_kernel,
        grid=(n // t,),
        in_specs=[pl.BlockSpec((t, D_MODEL), lambda i: (i, 0)),
                  pl.BlockSpec((None, 1, D_MODEL), lambda i: (layer, 0, 0)),
                  pl.BlockSpec((D_MODEL, LANES), lambda i: (0, 0)),
                  pl.BlockSpec((D_MODEL, LANES), lambda i: (0, 0))],
        out_specs=[pl.BlockSpec((t, D_MODEL), lambda i: (i, 0)),
                   pl.BlockSpec((t, LANES), lambda i: (i, 0)),
                   pl.BlockSpec((8, LANES), lambda i: (0, 0))],
        out_shape=[jax.ShapeDtypeStruct((n, D_MODEL), F32),
                   jax.ShapeDtypeStruct((n, LANES), F32),
                   jax.ShapeDtypeStruct((8, LANES), F32)],
        scratch_shapes=[pltpu.VMEM((1, LANES), F32)],
        compiler_params=_params(("arbitrary",)),
        name="route",
    )(h, g, rh, rl)


def _scatter_kernel(pos_ref, x_ref, dst_in_ref, dst_ref, sem):
    del dst_in_ref
    t = x_ref.shape[0]

    def row_copy(r, slot):
        return pltpu.make_async_copy(x_ref.at[pl.ds(r, 1)], dst_ref.at[pl.ds(slot, 1)], sem)

    def start(r, c):
        row_copy(r, pos_ref[r]).start()
        row_copy(r, pos_ref[t + r]).start()
        return c
    lax.fori_loop(0, t, start, 0)

    def wait(r, c):
        row_copy(0, 0).wait()
        row_copy(0, 0).wait()
        return c
    lax.fori_loop(0, t, wait, 0)


def _scatter_rows(xn, pos_tiles, padded_rows):
    n = xn.shape[0]
    t = SCATTER_TILE
    dst0 = jnp.zeros((padded_rows, D_MODEL), F32)
    return pl.pallas_call(
        _scatter_kernel,
        grid=(n // t,),
        in_specs=[pl.BlockSpec((2 * t,), lambda i: (i,), memory_space=pltpu.SMEM),
                  pl.BlockSpec((t, D_MODEL), lambda i: (i, 0)),
                  pl.BlockSpec(memory_space=pl.ANY)],
        out_specs=pl.BlockSpec(memory_space=pl.ANY),
        out_shape=jax.ShapeDtypeStruct((padded_rows, D_MODEL), F32),
        scratch_shapes=[pltpu.SemaphoreType.DMA(())],
        input_output_aliases={2: 0},
        compiler_params=_params(("arbitrary",)),
        name="scatter_rows",
    )(pos_tiles, xn, dst0)


def _combine_kernel(pos_ref, h_ref, info_ref, g_ref, ys_ref, o_ref, buf_ref, sem, *, final_norm):
    t = h_ref.shape[0]

    def row_copy(k, r, slot):
        return pltpu.make_async_copy(ys_ref.at[pl.ds(slot, 1)], buf_ref.at[k, pl.ds(r, 1)], sem)

    def start(r, c):
        row_copy(0, r, pos_ref[r]).start()
        row_copy(1, r, pos_ref[t + r]).start()
        return c
    lax.fori_loop(0, t, start, 0)

    def wait(r, c):
        row_copy(0, 0, 0).wait()
        row_copy(1, 0, 0).wait()
        return c
    lax.fori_loop(0, t, wait, 0)

    info = info_ref[...]
    lane = lax.broadcasted_iota(jnp.int32, info.shape, 1)
    g1 = jnp.sum(jnp.where(lane == 2, info, 0.0), axis=-1, keepdims=True)
    g2 = jnp.sum(jnp.where(lane == 3, info, 0.0), axis=-1, keepdims=True)
    y = h_ref[...] + g1 * buf_ref[0] + g2 * buf_ref[1]
    o_ref[...] = _rms_rows(y, g_ref[...]) if final_norm else y


def _combine(h, info, norm_final, ys, pos_tiles, final_norm):
    n = h.shape[0]
    t = COMBINE_TILE
    return pl.pallas_call(
        functools.partial(_combine_kernel, final_norm=final_norm),
        grid=(n // t,),
        in_specs=[pl.BlockSpec((2 * t,), lambda i: (i,), memory_space=pltpu.SMEM),
                  pl.BlockSpec((t, D_MODEL), lambda i: (i, 0)),
                  pl.BlockSpec((t, LANES), lambda i: (i, 0)),
                  pl.BlockSpec((1, D_MODEL), lambda i: (0, 0)),
                  pl.BlockSpec(memory_space=pl.ANY)],
        out_specs=pl.BlockSpec((t, D_MODEL), lambda i: (i, 0)),
        out_shape=jax.ShapeDtypeStruct((n, D_MODEL), F32),
        scratch_shapes=[pltpu.VMEM((2, t, D_MODEL), F32), pltpu.SemaphoreType.DMA(())],
        compiler_params=_params(("arbitrary",)),
        name="combine",
    )(pos_tiles, h, info, norm_final.reshape(1, D_MODEL), ys)


def _final_norm_kernel(h_ref, g_ref, o_ref):
    o_ref[...] = _rms_rows(h_ref[...], g_ref[...])


def _final_norm(h, norm_final):
    n = h.shape[0]
    t = 512
    return pl.pallas_call(
        _final_norm_kernel,
        grid=(n // t,),
        in_specs=[pl.BlockSpec((t, D_MODEL), lambda i: (i, 0)),
                  pl.BlockSpec((1, D_MODEL), lambda i: (0, 0))],
        out_specs=pl.BlockSpec((t, D_MODEL), lambda i: (i, 0)),
        out_shape=jax.ShapeDtypeStruct((n, D_MODEL), F32),
        compiler_params=_params(("parallel",)),
        name="final_norm",
    )(h, norm_final.reshape(1, D_MODEL))


def _pos_tiles(pos1, pos2, tile):
    nt = pos1.shape[0] // tile
    return jnp.stack([pos1.reshape(nt, tile), pos2.reshape(nt, tile)], axis=1).reshape(-1)


def _moe_layer(h, norm_g, layer, router, w1, w3, w2, moe_idx, norm_final, final_norm):
    n = h.shape[0]
    xn, info, cnt = _route(h, norm_g, router, layer)
    counts = cnt[0, :N_EXPERTS].astype(jnp.int32)
    padded = ((counts + ROW_TILE - 1) // ROW_TILE) * ROW_TILE
    ends = jnp.cumsum(padded)
    starts = ends - padded
    e1 = info[:, 0].astype(jnp.int32)
    e2 = info[:, 1].astype(jnp.int32)
    pos1 = starts[e1] + info[:, 4].astype(jnp.int32)
    pos2 = starts[e2] + info[:, 5].astype(jnp.int32)

    n_tiles = (n * TOP_K) // ROW_TILE + N_EXPERTS
    tile_start = jnp.arange(n_tiles, dtype=jnp.int32) * ROW_TILE
    tile_valid = (tile_start < ends[-1]).astype(jnp.int32)
    tile_expert = jnp.sum((tile_start[:, None] >= ends[None, :]).astype(jnp.int32), axis=1)
    last_expert = jnp.sum((ends[-1] - 1 >= ends).astype(jnp.int32))
    tile_expert = jnp.where(tile_valid > 0, tile_expert, last_expert).astype(jnp.int32)

    xs = _scatter_rows(xn, _pos_tiles(pos1, pos2, SCATTER_TILE), n_tiles * ROW_TILE)
    ys = _ffn(xs, norm_g, layer, w1, w3, w2, moe_idx, tile_expert, tile_valid, False)
    return _combine(h, info, norm_final, ys, _pos_tiles(pos1, pos2, COMBINE_TILE), final_norm)


def kernel(x, positions, norm_mix, norm_ffn, norm_final, w_in, conv_w, w_ret_out, w_conv_out,
           w_out, ffn_w1, ffn_w3, ffn_w2, router, moe_w1, moe_w3, moe_w2):
    bsz, seq, _ = x.shape
    depth = w_in.shape[0]
    n = bsz * seq
    h = x.reshape(n, D_MODEL)
    norm_mix = norm_mix.reshape(depth, 1, D_MODEL)
    norm_ffn = norm_ffn.reshape(depth, 1, D_MODEL)
    cos, sin = _rope_tables(positions)
    consts = _retention_consts()
    dense_tiles = n // ROW_TILE
    for layer in range(depth):
        proj = _inproj(h, norm_mix, w_in, layer)
        og = _retention(proj, cos, sin, consts, bsz, seq)
        cv = _short_conv(proj, conv_w, layer, bsz, seq)
        merged = _merge(og, cv, proj, w_ret_out, w_conv_out, layer)
        h = _outproj(merged, w_out, h, layer)
        i = layer // 2
        last = layer == depth - 1
        if layer % 2 == 0:
            h = _ffn(h, norm_ffn, layer, ffn_w1[:, None], ffn_w3[:, None], ffn_w2[:, None], i,
                     jnp.zeros((dense_tiles,), jnp.int32), jnp.ones((dense_tiles,), jnp.int32),
                     True)
            if last:
                h = _final_norm(h, norm_final)
        else:
            h = _moe_layer(h, norm_ffn, layer, router[i], moe_w1, moe_w3, moe_w2, i,
                           norm_final, last)
    return h.reshape(bsz, seq, D_MODEL)
```

```python
import functools

import jax
import jax.numpy as jnp
from jax import lax
from jax.experimental import pallas as pl
from jax.experimental.pallas import tpu as pltpu

F32 = jnp.float32
BF16 = jnp.bfloat16

D_MODEL = 2048
RET_HEADS = 4
RET_QK_DIM = 256
RET_V_DIM = 512
RET_BLOCK = 256
ROPE_BASE = 10000.0
CONV_K = 3
N_EXPERTS = 8
TOP_K = 2
EPS = 1e-6
HALF = RET_QK_DIM // 2

OFF_Q, OFF_K, OFF_V, OFF_G = 0, 1024, 2048, 4096
OFF_CB, OFF_CC, OFF_CH, OFF_GA, OFF_GB = 6144, 8192, 10240, 12288, 14336
D_IN_TOTAL = 16384

LANES = 128
VMEM_LIMIT = 56 * 1024 * 1024

ROW_TILE = 1024
SUB_ROWS = 256
MIX_TILE = 256
FFN_COLS = 256
ROUTE_TILE = 512
SCATTER_TILE = 512
COMBINE_TILE = 256


def _params(sem, vmem=VMEM_LIMIT):
    return pltpu.CompilerParams(dimension_semantics=sem, vmem_limit_bytes=vmem)


def _rms_rows(x, g):
    ms = jnp.mean(x * x, axis=-1, keepdims=True)
    return x * lax.rsqrt(ms + EPS) * g


def _sigmoid(x):
    return 1.0 / (1.0 + jnp.exp(-x))


def _rope_kernel(pos_ref, invf_ref, cos_ref, sin_ref):
    ang = pos_ref[...].astype(F32) * invf_ref[...]
    cos_ref[...] = jnp.cos(ang)
    sin_ref[...] = jnp.sin(ang)


def _rope_tables(positions):
    n = positions.size
    inv_freq = ROPE_BASE ** (-jnp.arange(HALF, dtype=F32) / HALF)
    rows = 2048
    return pl.pallas_call(
        _rope_kernel,
        grid=(n // rows,),
        in_specs=[pl.BlockSpec((rows, 1), lambda i: (i, 0)),
                  pl.BlockSpec((1, HALF), lambda i: (0, 0))],
        out_specs=[pl.BlockSpec((rows, HALF), lambda i: (i, 0)),
                   pl.BlockSpec((rows, HALF), lambda i: (i, 0))],
        out_shape=[jax.ShapeDtypeStruct((n, HALF), F32)] * 2,
        compiler_params=_params(("parallel",)),
        name="rope_tables",
    )(positions.reshape(n, 1), inv_freq.reshape(1, HALF))


def _norm_kernel(h_ref, g_ref, o_ref):
    o_ref[...] = _rms_rows(h_ref[...], g_ref[...]).astype(o_ref.dtype)


def _norm_rows(h, g, row, dtype):
    n = h.shape[0]
    t = 512
    return pl.pallas_call(
        _norm_kernel,
        grid=(n // t,),
        in_specs=[pl.BlockSpec((t, D_MODEL), lambda i: (i, 0)),
                  pl.BlockSpec((None, 1, D_MODEL), lambda i: (row, 0, 0))],
        out_specs=pl.BlockSpec((t, D_MODEL), lambda i: (i, 0)),
        out_shape=jax.ShapeDtypeStruct((n, D_MODEL), dtype),
        compiler_params=_params(("parallel",)),
        name="norm_rows",
    )(h, g)


def _inproj_kernel(x_ref, w_ref, o_ref, wb_ref):
    @pl.when(pl.program_id(1) == 0)
    def _():
        wb_ref[...] = w_ref[...].astype(BF16)

    o_ref[...] = jnp.dot(x_ref[...], wb_ref[...],
                         preferred_element_type=F32).astype(o_ref.dtype)


def _inproj(xn, w_in, layer):
    n = xn.shape[0]
    tn = 1024
    return pl.pallas_call(
        _inproj_kernel,
        grid=(D_IN_TOTAL // tn, n // ROW_TILE),
        in_specs=[pl.BlockSpec((ROW_TILE, D_MODEL), lambda j, i: (i, 0)),
                  pl.BlockSpec((None, D_MODEL, tn), lambda j, i: (layer, 0, j))],
        out_specs=pl.BlockSpec((ROW_TILE, tn), lambda j, i: (i, j)),
        out_shape=jax.ShapeDtypeStruct((n, D_IN_TOTAL), BF16),
        scratch_shapes=[pltpu.VMEM((D_MODEL, tn), BF16)],
        compiler_params=_params(("parallel", "arbitrary")),
        name="inproj",
    )(xn, w_in)


def _retention_kernel(gc_ref, q_ref, k_ref, v_ref, g_ref, cos_ref, sin_ref, dm_ref, xi_ref,
                      zeta_ref, o_ref, qr_ref, qx_ref, kr_ref, kz_ref, st_ref):
    seq = q_ref.shape[0]
    scale = RET_QK_DIM ** -0.5

    def rot(r, c):
        sl = pl.ds(pl.multiple_of(r * RET_BLOCK, RET_BLOCK), RET_BLOCK)
        cos = cos_ref[sl, :]
        sin = sin_ref[sl, :]
        xi = xi_ref[...]
        zeta = zeta_ref[...]
        q = q_ref[sl, :].astype(F32)
        q1, q2 = q[:, :HALF], q[:, HALF:]
        qa = (q1 * cos - q2 * sin) * scale
        qb = (q1 * sin + q2 * cos) * scale
        qr_ref[sl, :HALF] = qa.astype(BF16)
        qr_ref[sl, HALF:] = qb.astype(BF16)
        qx_ref[sl, :HALF] = (qa * xi).astype(BF16)
        qx_ref[sl, HALF:] = (qb * xi).astype(BF16)
        k = k_ref[sl, :].astype(F32)
        k1, k2 = k[:, :HALF], k[:, HALF:]
        ka = k1 * cos - k2 * sin
        kb = k1 * sin + k2 * cos
        kr_ref[sl, :HALF] = ka.astype(BF16)
        kr_ref[sl, HALF:] = kb.astype(BF16)
        kz_ref[sl, :HALF] = (ka * zeta).astype(BF16)
        kz_ref[sl, HALF:] = (kb * zeta).astype(BF16)
        return c
    lax.fori_loop(0, seq // RET_BLOCK, rot, 0)

    st_ref[...] = jnp.zeros_like(st_ref)
    gamma_block = gc_ref[pl.program_id(1)]

    def block(c, carry):
        sl = pl.ds(pl.multiple_of(c * RET_BLOCK, RET_BLOCK), RET_BLOCK)
        vc = v_ref[sl, :]
        scores = lax.dot_general(qr_ref[sl, :], kr_ref[sl, :], (((1,), (1,)), ((), ())),
                                 preferred_element_type=F32) * dm_ref[...]
        inner = jnp.dot(scores.astype(BF16), vc, preferred_element_type=F32)
        state = st_ref[...]
        cross = jnp.dot(qx_ref[sl, :], state.astype(BF16), preferred_element_type=F32)
        st_ref[...] = state * gamma_block + lax.dot_general(
            kz_ref[sl, :], vc, (((0,), (0,)), ((), ())), preferred_element_type=F32)
        o = inner + cross
        mu = jnp.mean(o, axis=-1, keepdims=True)
        d = o - mu
        var = jnp.mean(d * d, axis=-1, keepdims=True)
        on = d * lax.rsqrt(var + EPS)
        g = g_ref[sl, :].astype(F32)
        o_ref[sl, :] = (on * (g * _sigmoid(g))).astype(o_ref.dtype)
        return carry
    lax.fori_loop(0, seq // RET_BLOCK, block, 0, unroll=4)


def _retention_consts():
    h = RET_HEADS
    log_gamma = jnp.log1p(-jnp.power(2.0, -5.0 - jnp.arange(h, dtype=F32)))
    idx = jnp.arange(RET_BLOCK, dtype=F32)
    diff = idx[:, None] - idx[None, :]
    causal = diff >= 0
    decay_mask = jnp.where(causal[None],
                           jnp.exp(jnp.where(causal, diff, 0.0)[None] * log_gamma[:, None, None]),
                           0.0)
    xi = jnp.exp((idx + 1.0)[None] * log_gamma[:, None])
    zeta = jnp.exp((RET_BLOCK - 1.0 - idx)[None] * log_gamma[:, None])
    gamma_block = jnp.exp(RET_BLOCK * log_gamma)
    xi_b = jnp.broadcast_to(xi[:, :, None], (h, RET_BLOCK, HALF))
    zeta_b = jnp.broadcast_to(zeta[:, :, None], (h, RET_BLOCK, HALF))
    return decay_mask, xi_b, zeta_b, gamma_block


def _retention(proj, cos, sin, consts, bsz, seq):
    decay_mask, xi_b, zeta_b, gamma_chunk = consts
    n = proj.shape[0]
    qb, vb = RET_QK_DIM, RET_V_DIM
    return pl.pallas_call(
        _retention_kernel,
        grid=(bsz, RET_HEADS),
        in_specs=[
            pl.BlockSpec(memory_space=pltpu.SMEM),
            pl.BlockSpec((seq, qb), lambda b, h: (b, OFF_Q // qb + h)),
            pl.BlockSpec((seq, qb), lambda b, h: (b, OFF_K // qb + h)),
            pl.BlockSpec((seq, vb), lambda b, h: (b, OFF_V // vb + h)),
            pl.BlockSpec((seq, vb), lambda b, h: (b, OFF_G // vb + h)),
            pl.BlockSpec((seq, HALF), lambda b, h: (b, 0)),
            pl.BlockSpec((seq, HALF), lambda b, h: (b, 0)),
            pl.BlockSpec((None, RET_BLOCK, RET_BLOCK), lambda b, h: (h, 0, 0)),
            pl.BlockSpec((None, RET_BLOCK, HALF), lambda b, h: (h, 0, 0)),
            pl.BlockSpec((None, RET_BLOCK, HALF), lambda b, h: (h, 0, 0)),
        ],
        out_specs=pl.BlockSpec((seq, vb), lambda b, h: (b, h)),
        scratch_shapes=[pltpu.VMEM((seq, qb), BF16), pltpu.VMEM((seq, qb), BF16),
                        pltpu.VMEM((seq, qb), BF16), pltpu.VMEM((seq, qb), BF16),
                        pltpu.VMEM((qb, vb), F32)],
        out_shape=jax.ShapeDtypeStruct((n, RET_HEADS * vb), BF16),
        compiler_params=_params(("parallel", "parallel")),
        name="retention",
    )(gamma_chunk, proj, proj, proj, proj, cos, sin, decay_mask, xi_b, zeta_b)


def _conv_kernel(cb_ref, cc_ref, ch_ref, w_ref, o_ref):
    u = cc_ref[...].astype(F32) * ch_ref[...].astype(F32)
    row = lax.broadcasted_iota(jnp.int32, u.shape, 0)
    u1 = jnp.where(row >= 1, pltpu.roll(u, 1, 0), 0.0)
    u2 = jnp.where(row >= 2, pltpu.roll(u, 2, 0), 0.0)
    w = w_ref[...]
    y = w[0:1, :] * u2 + w[1:2, :] * u1 + w[2:3, :] * u
    o_ref[...] = (cb_ref[...].astype(F32) * y).astype(o_ref.dtype)


def _short_conv(proj, conv_w, layer, bsz, seq):
    n = proj.shape[0]
    cw = 256
    return pl.pallas_call(
        _conv_kernel,
        grid=(bsz, D_MODEL // cw),
        in_specs=[pl.BlockSpec((seq, cw), lambda b, j: (b, OFF_CB // cw + j)),
                  pl.BlockSpec((seq, cw), lambda b, j: (b, OFF_CC // cw + j)),
                  pl.BlockSpec((seq, cw), lambda b, j: (b, OFF_CH // cw + j)),
                  pl.BlockSpec((None, CONV_K, cw), lambda b, j: (layer, 0, j))],
        out_specs=pl.BlockSpec((seq, cw), lambda b, j: (b, j)),
        out_shape=jax.ShapeDtypeStruct((n, D_MODEL), BF16),
        compiler_params=_params(("parallel", "parallel")),
        name="short_conv",
    )(proj, proj, proj, conv_w)


def _mix_out_kernel(og_ref, cv_ref, ga_ref, gb_ref, h_ref, wr_ref, wc_ref, wo_ref, o_ref):
    yr = jnp.dot(og_ref[...], wr_ref[...], preferred_element_type=F32)
    yc = jnp.dot(cv_ref[...], wc_ref[...], preferred_element_type=F32)
    ga = ga_ref[...].astype(F32)
    gb = gb_ref[...].astype(F32)
    merged = (_sigmoid(ga) * yr + _sigmoid(gb) * yc).astype(BF16)
    o_ref[...] = h_ref[...] + jnp.dot(merged, wo_ref[...], preferred_element_type=F32)


def _mix_out(og, cv, proj, h, w_ret_out, w_conv_out, w_out, layer):
    n = h.shape[0]
    t = MIX_TILE
    gcol = D_MODEL
    rows = lambda i: (i, 0)
    wspec = pl.BlockSpec((None, D_MODEL, D_MODEL), lambda i: (layer, 0, 0),
                         pipeline_mode=pl.Buffered(1))
    return pl.pallas_call(
        _mix_out_kernel,
        grid=(n // t,),
        in_specs=[pl.BlockSpec((t, D_MODEL), rows),
                  pl.BlockSpec((t, D_MODEL), rows),
                  pl.BlockSpec((t, gcol), lambda i: (i, OFF_GA // gcol)),
                  pl.BlockSpec((t, gcol), lambda i: (i, OFF_GB // gcol)),
                  pl.BlockSpec((t, D_MODEL), rows),
                  wspec, wspec, wspec],
        out_specs=pl.BlockSpec((t, D_MODEL), rows),
        out_shape=jax.ShapeDtypeStruct((n, D_MODEL), F32),
        compiler_params=_params(("parallel",)),
        name="mix_out",
    )(og, cv, proj, proj, h, w_ret_out, w_conv_out, w_out)


def _ffn_kernel(te_ref, tv_ref, x_ref, g_ref, w1_ref, w3_ref, w2_ref, o_ref,
                xb_ref, *, fused_norm_residual):
    i = pl.program_id(0)
    f = pl.program_id(1)
    n_sub = x_ref.shape[0] // SUB_ROWS

    @pl.when(f == 0)
    def _():
        def body(r, c):
            sl = pl.ds(pl.multiple_of(r * SUB_ROWS, SUB_ROWS), SUB_ROWS)
            x = x_ref[sl, :]
            if fused_norm_residual:
                xb_ref[sl, :] = _rms_rows(x, g_ref[...]).astype(BF16)
                o_ref[sl, :] = x
            else:
                xb_ref[sl, :] = x.astype(BF16)
                o_ref[sl, :] = jnp.zeros_like(x)
            return c
        lax.fori_loop(0, n_sub, body, 0)

    def swiglu_rows(rows):
        x = xb_ref[:rows, :]
        a = jnp.dot(x, w1_ref[...].astype(BF16), preferred_element_type=F32)
        b = jnp.dot(x, w3_ref[...].astype(BF16), preferred_element_type=F32)
        hh = (a * _sigmoid(a) * b).astype(BF16)
        o_ref[:rows, :] += jnp.dot(hh, w2_ref[...].astype(BF16), preferred_element_type=F32)

    if fused_norm_residual:
        swiglu_rows(n_sub * SUB_ROWS)
    else:
        for groups in range(1, n_sub + 1):
            pl.when(tv_ref[i] == groups)(functools.partial(swiglu_rows, groups * SUB_ROWS))


def _ffn(x, g, g_row, w1, w3, w2, w_idx, tile_expert, tile_valid, fused_norm_residual):
    rows = x.shape[0]
    d_ff = w1.shape[-1]
    tf = FFN_COLS
    nf = d_ff // tf
    n_tiles = rows // ROW_TILE

    def wcol(i, f, te, tv):
        return jnp.where(tv[i] > 0, f, nf - 1)

    kern = functools.partial(_ffn_kernel, fused_norm_residual=fused_norm_residual)
    return pl.pallas_call(
        kern,
        grid_spec=pltpu.PrefetchScalarGridSpec(
            num_scalar_prefetch=2,
            grid=(n_tiles, nf),
            in_specs=[
                pl.BlockSpec((ROW_TILE, D_MODEL), lambda i, f, te, tv: (i, 0),
                             pipeline_mode=pl.Buffered(1)),
                pl.BlockSpec((None, 1, D_MODEL), lambda i, f, te, tv: (g_row, 0, 0)),
                pl.BlockSpec((None, None, D_MODEL, tf),
                             lambda i, f, te, tv: (w_idx, te[i], 0, wcol(i, f, te, tv))),
                pl.BlockSpec((None, None, D_MODEL, tf),
                             lambda i, f, te, tv: (w_idx, te[i], 0, wcol(i, f, te, tv))),
                pl.BlockSpec((None, None, tf, D_MODEL),
                             lambda i, f, te, tv: (w_idx, te[i], wcol(i, f, te, tv), 0)),
            ],
            out_specs=pl.BlockSpec((ROW_TILE, D_MODEL), lambda i, f, te, tv: (i, 0)),
            scratch_shapes=[pltpu.VMEM((ROW_TILE, D_MODEL), BF16)],
        ),
        out_shape=jax.ShapeDtypeStruct((rows, D_MODEL), F32),
        compiler_params=_params(("parallel", "arbitrary")),
        name="ffn_dense" if fused_norm_residual else "ffn_experts",
    )(tile_expert, tile_valid, x, g, w1, w3, w2)


def _route_kernel(h_ref, g_ref, rh_ref, rl_ref, xn_ref, info_ref, cnt_ref, run_ref):
    t = h_ref.shape[0]

    @pl.when(pl.program_id(0) == 0)
    def _():
        run_ref[...] = jnp.zeros_like(run_ref)

    xn = _rms_rows(h_ref[...], g_ref[...])
    xn_ref[...] = xn
    xh = xn.astype(BF16)
    xl = (xn - xh.astype(F32)).astype(BF16)
    logits = (jnp.dot(xh, rh_ref[...], preferred_element_type=F32)
              + jnp.dot(xl, rh_ref[...], preferred_element_type=F32)
              + jnp.dot(xh, rl_ref[...], preferred_element_type=F32))
    lane = lax.broadcasted_iota(jnp.int32, (t, LANES), 1).astype(F32)
    neg = jnp.float32(-jnp.inf)
    logits = jnp.where(lane < N_EXPERTS, logits, neg)
    v1 = jnp.max(logits, axis=-1, keepdims=True)
    e1 = jnp.min(jnp.where(logits == v1, lane, float(LANES)), axis=-1, keepdims=True)
    rest = jnp.where(lane == e1, neg, logits)
    v2 = jnp.max(rest, axis=-1, keepdims=True)
    e2 = jnp.min(jnp.where(rest == v2, lane, float(LANES)), axis=-1, keepdims=True)
    ex = jnp.exp(v2 - v1)
    g1 = 1.0 / (1.0 + ex)
    g2 = ex / (1.0 + ex)

    onehot = jnp.where(jnp.logical_or(lane == e1, lane == e2), 1.0, 0.0)
    r_i = lax.broadcasted_iota(jnp.int32, (t, t), 0)
    c_i = lax.broadcasted_iota(jnp.int32, (t, t), 1)
    tri = jnp.where(c_i <= r_i, 1.0, 0.0).astype(BF16)
    cum = jnp.dot(tri, onehot.astype(BF16), preferred_element_type=F32)
    excl = cum - onehot + run_ref[...]
    rank1 = jnp.sum(jnp.where(lane == e1, excl, 0.0), axis=-1, keepdims=True)
    rank2 = jnp.sum(jnp.where(lane == e2, excl, 0.0), axis=-1, keepdims=True)
    run_ref[...] = run_ref[...] + cum[t - 1:t, :]

    info = jnp.where(lane == 0, e1, 0.0)
    info = jnp.where(lane == 1, e2, info)
    info = jnp.where(lane == 2, g1, info)
    info = jnp.where(lane == 3, g2, info)
    info = jnp.where(lane == 4, rank1, info)
    info = jnp.where(lane == 5, rank2, info)
    info_ref[...] = info
    cnt_ref[...] = jnp.broadcast_to(run_ref[...], cnt_ref.shape)


def _route(h, g, router, layer):
    n = h.shape[0]
    r = jnp.zeros((D_MODEL, LANES), F32).at[:, :N_EXPERTS].set(router)
    rh = r.astype(BF16)
    rl = (r - rh.astype(F32)).astype(BF16)
    t = ROUTE_TILE
    return pl.pallas_call(
        _route_kernel,
        grid=(n // t,),
        in_specs=[pl.BlockSpec((t, D_MODEL), lambda i: (i, 0)),
                  pl.BlockSpec((None, 1, D_MODEL), lambda i: (layer, 0, 0)),
                  pl.BlockSpec((D_MODEL, LANES), lambda i: (0, 0)),
                  pl.BlockSpec((D_MODEL, LANES), lambda i: (0, 0))],
        out_specs=[pl.BlockSpec((t, D_MODEL), lambda i: (i, 0)),
                   pl.BlockSpec((t, LANES), lambda i: (i, 0)),
                   pl.BlockSpec((8, LANES), lambda i: (0, 0))],
        out_shape=[jax.ShapeDtypeStruct((n, D_MODEL), F32),
                   jax.ShapeDtypeStruct((n, LANES), F32),
                   jax.ShapeDtypeStruct((8, LANES), F32)],
        scratch_shapes=[pltpu.VMEM((1, LANES), F32)],
        compiler_params=_params(("arbitrary",)),
        name="route",
    )(h, g, rh, rl)


def _scatter_kernel(pos_ref, x_ref, dst_in_ref, dst_ref, sem):
    del dst_in_ref
    t = x_ref.shape[0]

    def row_copy(r, slot):
        return pltpu.make_async_copy(x_ref.at[pl.ds(r, 1)], dst_ref.at[pl.ds(slot, 1)], sem)

    def start(r, c):
        row_copy(r, pos_ref[r]).start()
        row_copy(r, pos_ref[t + r]).start()
        return c
    lax.fori_loop(0, t, start, 0)

    def wait(r, c):
        row_copy(0, 0).wait()
        row_copy(0, 0).wait()
        return c
    lax.fori_loop(0, t, wait, 0)


def _scatter_rows(xn, pos_tiles, padded_rows):
    n = xn.shape[0]
    t = SCATTER_TILE
    dst0 = jnp.zeros((padded_rows, D_MODEL), F32)
    return pl.pallas_call(
        _scatter_kernel,
        grid=(n // t,),
        in_specs=[pl.BlockSpec((2 * t,), lambda i: (i,), memory_space=pltpu.SMEM),
                  pl.BlockSpec((t, D_MODEL), lambda i: (i, 0)),
                  pl.BlockSpec(memory_space=pl.ANY)],
        out_specs=pl.BlockSpec(memory_space=pl.ANY),
        out_shape=jax.ShapeDtypeStruct((padded_rows, D_MODEL), F32),
        scratch_shapes=[pltpu.SemaphoreType.DMA(())],
        input_output_aliases={2: 0},
        compiler_params=_params(("arbitrary",)),
        name="scatter_rows",
    )(pos_tiles, xn, dst0)


def _combine_kernel(pos_ref, h_ref, info_ref, g_ref, ys_ref, o_ref, buf_ref, sem, *, final_norm):
    t = h_ref.shape[0]

    def row_copy(k, r, slot):
        return pltpu.make_async_copy(ys_ref.at[pl.ds(slot, 1)], buf_ref.at[k, pl.ds(r, 1)], sem)

    def start(r, c):
        row_copy(0, r, pos_ref[r]).start()
        row_copy(1, r, pos_ref[t + r]).start()
        return c
    lax.fori_loop(0, t, start, 0)

    def wait(r, c):
        row_copy(0, 0, 0).wait()
        row_copy(1, 0, 0).wait()
        return c
    lax.fori_loop(0, t, wait, 0)

    info = info_ref[...]
    lane = lax.broadcasted_iota(jnp.int32, info.shape, 1)
    g1 = jnp.sum(jnp.where(lane == 2, info, 0.0), axis=-1, keepdims=True)
    g2 = jnp.sum(jnp.where(lane == 3, info, 0.0), axis=-1, keepdims=True)
    y = h_ref[...] + g1 * buf_ref[0] + g2 * buf_ref[1]
    o_ref[...] = _rms_rows(y, g_ref[...]) if final_norm else y


def _combine(h, info, norm_final, ys, pos_tiles, final_norm):
    n = h.shape[0]
    t = COMBINE_TILE
    return pl.pallas_call(
        functools.partial(_combine_kernel, final_norm=final_norm),
        grid=(n // t,),
        in_specs=[pl.BlockSpec((2 * t,), lambda i: (i,), memory_space=pltpu.SMEM),
                  pl.BlockSpec((t, D_MODEL), lambda i: (i, 0)),
                  pl.BlockSpec((t, LANES), lambda i: (i, 0)),
                  pl.BlockSpec((1, D_MODEL), lambda i: (0, 0)),
                  pl.BlockSpec(memory_space=pl.ANY)],
        out_specs=pl.BlockSpec((t, D_MODEL), lambda i: (i, 0)),
        out_shape=jax.ShapeDtypeStruct((n, D_MODEL), F32),
        scratch_shapes=[pltpu.VMEM((2, t, D_MODEL), F32), pltpu.SemaphoreType.DMA(())],
        compiler_params=_params(("arbitrary",)),
        name="combine",
    )(pos_tiles, h, info, norm_final.reshape(1, D_MODEL), ys)


def _pos_tiles(pos1, pos2, tile):
    nt = pos1.shape[0] // tile
    return jnp.stack([pos1.reshape(nt, tile), pos2.reshape(nt, tile)], axis=1).reshape(-1)


def _moe_layer(h, norm_g, layer, router, w1, w3, w2, moe_idx, norm_final, final_norm):
    n = h.shape[0]
    xn, info, cnt = _route(h, norm_g, router, layer)
    counts = cnt[0, :N_EXPERTS].astype(jnp.int32)
    padded = ((counts + ROW_TILE - 1) // ROW_TILE) * ROW_TILE
    ends = jnp.cumsum(padded)
    starts = ends - padded
    e1 = info[:, 0].astype(jnp.int32)
    e2 = info[:, 1].astype(jnp.int32)
    pos1 = starts[e1] + info[:, 4].astype(jnp.int32)
    pos2 = starts[e2] + info[:, 5].astype(jnp.int32)

    n_tiles = (n * TOP_K) // ROW_TILE + N_EXPERTS
    tile_start = jnp.arange(n_tiles, dtype=jnp.int32) * ROW_TILE
    in_use = tile_start < ends[-1]
    tile_expert = jnp.sum((tile_start[:, None] >= ends[None, :]).astype(jnp.int32), axis=1)
    last_expert = jnp.sum((ends[-1] - 1 >= ends).astype(jnp.int32))
    tile_expert = jnp.where(in_use, tile_expert, last_expert).astype(jnp.int32)
    tile_rows = jnp.clip((starts + counts)[tile_expert] - tile_start, 0, ROW_TILE)
    tile_valid = jnp.where(in_use, (tile_rows + SUB_ROWS - 1) // SUB_ROWS, 0).astype(jnp.int32)

    xs = _scatter_rows(xn, _pos_tiles(pos1, pos2, SCATTER_TILE), n_tiles * ROW_TILE)
    ys = _ffn(xs, norm_g, layer, w1, w3, w2, moe_idx, tile_expert, tile_valid, False)
    return _combine(h, info, norm_final, ys, _pos_tiles(pos1, pos2, COMBINE_TILE), final_norm)


def kernel(x, positions, norm_mix, norm_ffn, norm_final, w_in, conv_w, w_ret_out, w_conv_out,
           w_out, ffn_w1, ffn_w3, ffn_w2, router, moe_w1, moe_w3, moe_w2):
    bsz, seq, _ = x.shape
    depth = w_in.shape[0]
    n = bsz * seq
    h = x.reshape(n, D_MODEL)
    norm_mix = norm_mix.reshape(depth, 1, D_MODEL)
    norm_ffn = norm_ffn.reshape(depth, 1, D_MODEL)
    cos, sin = _rope_tables(positions)
    consts = _retention_consts()
    dense_tiles = n // ROW_TILE
    w_ret_out, w_conv_out, w_out = (w.astype(BF16) for w in (w_ret_out, w_conv_out, w_out))
    for layer in range(depth):
        proj = _inproj(_norm_rows(h, norm_mix, layer, BF16), w_in, layer)
        og = _retention(proj, cos, sin, consts, bsz, seq)
        cv = _short_conv(proj, conv_w, layer, bsz, seq)
        h = _mix_out(og, cv, proj, h, w_ret_out, w_conv_out, w_out, layer)
        i = layer // 2
        last = layer == depth - 1
        if layer % 2 == 0:
            h = _ffn(h, norm_ffn, layer, ffn_w1[:, None], ffn_w3[:, None], ffn_w2[:, None], i,
                     jnp.zeros((dense_tiles,), jnp.int32),
                     jnp.full((dense_tiles,), ROW_TILE // SUB_ROWS, jnp.int32), True)
            if last:
                h = _norm_rows(h, norm_final.reshape(1, 1, D_MODEL), 0, F32)
        else:
            h = _moe_layer(h, norm_ffn, layer, router[i], moe_w1, moe_w3, moe_w2, i,
                           norm_final, last)
    return h.reshape(bsz, seq, D_MODEL)
```

```python
import functools

import jax
import jax.numpy as jnp
from jax import lax
from jax.experimental import pallas as pl
from jax.experimental.pallas import tpu as pltpu

F32 = jnp.float32
BF16 = jnp.bfloat16

D_MODEL = 2048
RET_HEADS = 4
RET_QK_DIM = 256
RET_V_DIM = 512
RET_BLOCK = 256
ROPE_BASE = 10000.0
CONV_K = 3
N_EXPERTS = 8
TOP_K = 2
EPS = 1e-6
HALF = RET_QK_DIM // 2

OFF_Q, OFF_K, OFF_V, OFF_G = 0, 1024, 2048, 4096
OFF_CB, OFF_CC, OFF_CH, OFF_GA, OFF_GB = 6144, 8192, 10240, 12288, 14336
D_IN_TOTAL = 16384

LANES = 128
VMEM_LIMIT = 56 * 1024 * 1024

ROW_TILE = 1024
SUB_ROWS = 256
MIX_TILE = 256
FFN_COLS = 256
ROUTE_TILE = 512
CONV_LANES = 512
COMBINE_TILE = 256


def _params(sem, vmem=VMEM_LIMIT):
    return pltpu.CompilerParams(dimension_semantics=sem, vmem_limit_bytes=vmem)


def _rms_rows(x, g):
    ms = jnp.mean(x * x, axis=-1, keepdims=True)
    return x * lax.rsqrt(ms + EPS) * g


def _sigmoid(x):
    return 1.0 / (1.0 + jnp.exp(-x))


def _rope_kernel(pos_ref, invf_ref, cos_ref, sin_ref):
    ang = pos_ref[...].astype(F32) * invf_ref[...]
    cos_ref[...] = jnp.cos(ang)
    sin_ref[...] = jnp.sin(ang)


def _rope_tables(positions):
    n = positions.size
    inv_freq = ROPE_BASE ** (-jnp.arange(HALF, dtype=F32) / HALF)
    rows = 2048
    return pl.pallas_call(
        _rope_kernel,
        grid=(n // rows,),
        in_specs=[pl.BlockSpec((rows, 1), lambda i: (i, 0)),
                  pl.BlockSpec((1, HALF), lambda i: (0, 0))],
        out_specs=[pl.BlockSpec((rows, HALF), lambda i: (i, 0)),
                   pl.BlockSpec((rows, HALF), lambda i: (i, 0))],
        out_shape=[jax.ShapeDtypeStruct((n, HALF), F32)] * 2,
        compiler_params=_params(("parallel",)),
        name="rope_tables",
    )(positions.reshape(n, 1), inv_freq.reshape(1, HALF))


def _norm_kernel(h_ref, g_ref, o_ref):
    o_ref[...] = _rms_rows(h_ref[...], g_ref[...]).astype(o_ref.dtype)


def _norm_rows(h, g, row, dtype):
    n = h.shape[0]
    t = 512
    return pl.pallas_call(
        _norm_kernel,
        grid=(n // t,),
        in_specs=[pl.BlockSpec((t, D_MODEL), lambda i: (i, 0)),
                  pl.BlockSpec((None, 1, D_MODEL), lambda i: (row, 0, 0))],
        out_specs=pl.BlockSpec((t, D_MODEL), lambda i: (i, 0)),
        out_shape=jax.ShapeDtypeStruct((n, D_MODEL), dtype),
        compiler_params=_params(("parallel",)),
        name="norm_rows",
    )(h, g)


def _inproj_kernel(x_ref, w_ref, o_ref, wb_ref):
    @pl.when(pl.program_id(1) == 0)
    def _():
        wb_ref[...] = w_ref[...].astype(BF16)

    o_ref[...] = jnp.dot(x_ref[...], wb_ref[...],
                         preferred_element_type=F32).astype(o_ref.dtype)


def _inproj(xn, w_in, layer):
    n = xn.shape[0]
    tn = 1024
    return pl.pallas_call(
        _inproj_kernel,
        grid=(D_IN_TOTAL // tn, n // ROW_TILE),
        in_specs=[pl.BlockSpec((ROW_TILE, D_MODEL), lambda j, i: (i, 0)),
                  pl.BlockSpec((None, D_MODEL, tn), lambda j, i: (layer, 0, j))],
        out_specs=pl.BlockSpec((ROW_TILE, tn), lambda j, i: (i, j)),
        out_shape=jax.ShapeDtypeStruct((n, D_IN_TOTAL), BF16),
        scratch_shapes=[pltpu.VMEM((D_MODEL, tn), BF16)],
        compiler_params=_params(("parallel", "arbitrary")),
        name="inproj",
    )(xn, w_in)


def _retention_kernel(gc_ref, q_ref, k_ref, v_ref, g_ref, cos_ref, sin_ref, dm_ref, xi_ref,
                      zeta_ref, o_ref, qr_ref, qx_ref, kr_ref, kz_ref, st_ref):
    seq = q_ref.shape[0]
    scale = RET_QK_DIM ** -0.5

    def rot(r, c):
        sl = pl.ds(pl.multiple_of(r * RET_BLOCK, RET_BLOCK), RET_BLOCK)
        cos = cos_ref[sl, :]
        sin = sin_ref[sl, :]
        xi = xi_ref[...]
        zeta = zeta_ref[...]
        q = q_ref[sl, :].astype(F32)
        q1, q2 = q[:, :HALF], q[:, HALF:]
        qa = (q1 * cos - q2 * sin) * scale
        qb = (q1 * sin + q2 * cos) * scale
        qr_ref[sl, :HALF] = qa.astype(BF16)
        qr_ref[sl, HALF:] = qb.astype(BF16)
        qx_ref[sl, :HALF] = (qa * xi).astype(BF16)
        qx_ref[sl, HALF:] = (qb * xi).astype(BF16)
        k = k_ref[sl, :].astype(F32)
        k1, k2 = k[:, :HALF], k[:, HALF:]
        ka = k1 * cos - k2 * sin
        kb = k1 * sin + k2 * cos
        kr_ref[sl, :HALF] = ka.astype(BF16)
        kr_ref[sl, HALF:] = kb.astype(BF16)
        kz_ref[sl, :HALF] = (ka * zeta).astype(BF16)
        kz_ref[sl, HALF:] = (kb * zeta).astype(BF16)
        return c
    lax.fori_loop(0, seq // RET_BLOCK, rot, 0)

    st_ref[...] = jnp.zeros_like(st_ref)
    gamma_block = gc_ref[pl.program_id(1)]

    def block(c, carry):
        sl = pl.ds(pl.multiple_of(c * RET_BLOCK, RET_BLOCK), RET_BLOCK)
        vc = v_ref[sl, :]
        scores = lax.dot_general(qr_ref[sl, :], kr_ref[sl, :], (((1,), (1,)), ((), ())),
                                 preferred_element_type=F32) * dm_ref[...]
        inner = jnp.dot(scores.astype(BF16), vc, preferred_element_type=F32)
        state = st_ref[...]
        cross = jnp.dot(qx_ref[sl, :], state.astype(BF16), preferred_element_type=F32)
        st_ref[...] = state * gamma_block + lax.dot_general(
            kz_ref[sl, :], vc, (((0,), (0,)), ((), ())), preferred_element_type=F32)
        o = inner + cross
        mu = jnp.mean(o, axis=-1, keepdims=True)
        d = o - mu
        var = jnp.mean(d * d, axis=-1, keepdims=True)
        on = d * lax.rsqrt(var + EPS)
        g = g_ref[sl, :].astype(F32)
        o_ref[sl, :] = (on * (g * _sigmoid(g))).astype(o_ref.dtype)
        return carry
    lax.fori_loop(0, seq // RET_BLOCK, block, 0, unroll=4)


def _retention_consts():
    h = RET_HEADS
    log_gamma = jnp.log1p(-jnp.power(2.0, -5.0 - jnp.arange(h, dtype=F32)))
    idx = jnp.arange(RET_BLOCK, dtype=F32)
    diff = idx[:, None] - idx[None, :]
    causal = diff >= 0
    decay_mask = jnp.where(causal[None],
                           jnp.exp(jnp.where(causal, diff, 0.0)[None] * log_gamma[:, None, None]),
                           0.0)
    xi = jnp.exp((idx + 1.0)[None] * log_gamma[:, None])
    zeta = jnp.exp((RET_BLOCK - 1.0 - idx)[None] * log_gamma[:, None])
    gamma_block = jnp.exp(RET_BLOCK * log_gamma)
    xi_b = jnp.broadcast_to(xi[:, :, None], (h, RET_BLOCK, HALF))
    zeta_b = jnp.broadcast_to(zeta[:, :, None], (h, RET_BLOCK, HALF))
    return decay_mask, xi_b, zeta_b, gamma_block


def _retention(proj, cos, sin, consts, bsz, seq):
    decay_mask, xi_b, zeta_b, gamma_chunk = consts
    n = proj.shape[0]
    qb, vb = RET_QK_DIM, RET_V_DIM
    return pl.pallas_call(
        _retention_kernel,
        grid=(bsz, RET_HEADS),
        in_specs=[
            pl.BlockSpec(memory_space=pltpu.SMEM),
            pl.BlockSpec((seq, qb), lambda b, h: (b, OFF_Q // qb + h)),
            pl.BlockSpec((seq, qb), lambda b, h: (b, OFF_K // qb + h)),
            pl.BlockSpec((seq, vb), lambda b, h: (b, OFF_V // vb + h)),
            pl.BlockSpec((seq, vb), lambda b, h: (b, OFF_G // vb + h)),
            pl.BlockSpec((seq, HALF), lambda b, h: (b, 0)),
            pl.BlockSpec((seq, HALF), lambda b, h: (b, 0)),
            pl.BlockSpec((None, RET_BLOCK, RET_BLOCK), lambda b, h: (h, 0, 0)),
            pl.BlockSpec((None, RET_BLOCK, HALF), lambda b, h: (h, 0, 0)),
            pl.BlockSpec((None, RET_BLOCK, HALF), lambda b, h: (h, 0, 0)),
        ],
        out_specs=pl.BlockSpec((seq, vb), lambda b, h: (b, h)),
        scratch_shapes=[pltpu.VMEM((seq, qb), BF16), pltpu.VMEM((seq, qb), BF16),
                        pltpu.VMEM((seq, qb), BF16), pltpu.VMEM((seq, qb), BF16),
                        pltpu.VMEM((qb, vb), F32)],
        out_shape=jax.ShapeDtypeStruct((n, RET_HEADS * vb), BF16),
        compiler_params=_params(("parallel", "parallel")),
        name="retention",
    )(gamma_chunk, proj, proj, proj, proj, cos, sin, decay_mask, xi_b, zeta_b)


def _mix_out_kernel(og_ref, cb_ref, cc_ref, ch_ref, ga_ref, gb_ref, h_ref, cw_ref,
                    wr_ref, wc_ref, wo_ref, o_ref, cv_ref, tail_ref, *, tiles_per_seq):
    t = og_ref.shape[0]

    @pl.when(pl.program_id(0) % tiles_per_seq == 0)
    def _():
        tail_ref[...] = jnp.zeros_like(tail_ref)

    row = lax.broadcasted_iota(jnp.int32, (t, CONV_LANES), 0)
    for c in range(D_MODEL // CONV_LANES):
        cs = slice(c * CONV_LANES, (c + 1) * CONV_LANES)
        u = cc_ref[:, cs].astype(F32) * ch_ref[:, cs].astype(F32)
        prev1 = tail_ref[7:8, cs]
        prev2 = tail_ref[6:7, cs]
        u1 = jnp.where(row >= 1, pltpu.roll(u, 1, 0), prev1)
        u2 = jnp.where(row >= 2, pltpu.roll(u, 2, 0), jnp.where(row == 1, prev1, prev2))
        w = cw_ref[:, cs]
        y = w[0:1, :] * u2 + w[1:2, :] * u1 + w[2:3, :] * u
        cv_ref[:, cs] = (cb_ref[:, cs].astype(F32) * y).astype(BF16)
        tail_ref[:, cs] = u[t - 8:t, :]

    yr = jnp.dot(og_ref[...], wr_ref[...], preferred_element_type=F32)
    yc = jnp.dot(cv_ref[...], wc_ref[...], preferred_element_type=F32)
    ga = ga_ref[...].astype(F32)
    gb = gb_ref[...].astype(F32)
    merged = (_sigmoid(ga) * yr + _sigmoid(gb) * yc).astype(BF16)
    o_ref[...] = h_ref[...] + jnp.dot(merged, wo_ref[...], preferred_element_type=F32)


def _mix_out(og, proj, h, conv_w, w_ret_out, w_conv_out, w_out, layer, seq):
    n = h.shape[0]
    t = MIX_TILE
    rows = lambda i: (i, 0)
    pcol = lambda off: pl.BlockSpec((t, D_MODEL), lambda i: (i, off // D_MODEL))
    wspec = pl.BlockSpec((None, D_MODEL, D_MODEL), lambda i: (layer, 0, 0),
                         pipeline_mode=pl.Buffered(1))
    return pl.pallas_call(
        functools.partial(_mix_out_kernel, tiles_per_seq=seq // t),
        grid=(n // t,),
        in_specs=[pl.BlockSpec((t, D_MODEL), rows),
                  pcol(OFF_CB), pcol(OFF_CC), pcol(OFF_CH), pcol(OFF_GA), pcol(OFF_GB),
                  pl.BlockSpec((t, D_MODEL), rows),
                  pl.BlockSpec((None, CONV_K, D_MODEL), lambda i: (layer, 0, 0)),
                  wspec, wspec, wspec],
        out_specs=pl.BlockSpec((t, D_MODEL), rows),
        out_shape=jax.ShapeDtypeStruct((n, D_MODEL), F32),
        scratch_shapes=[pltpu.VMEM((t, D_MODEL), BF16), pltpu.VMEM((8, D_MODEL), F32)],
        compiler_params=_params(("arbitrary",)),
        name="mix_out",
    )(og, proj, proj, proj, proj, proj, h, conv_w, w_ret_out, w_conv_out, w_out)


def _normalise_tile(src_ref, g_ref, xb_ref, o_ref, residual):
    def body(r, c):
        sl = pl.ds(pl.multiple_of(r * SUB_ROWS, SUB_ROWS), SUB_ROWS)
        x = src_ref[sl, :]
        xb_ref[sl, :] = _rms_rows(x, g_ref[...]).astype(BF16)
        o_ref[sl, :] = x if residual else jnp.zeros_like(x)
        return c
    lax.fori_loop(0, ROW_TILE // SUB_ROWS, body, 0)


def _swiglu_rows(xb_ref, w1_ref, w3_ref, w2_ref, o_ref, rows):
    x = xb_ref[:rows, :]
    a = jnp.dot(x, w1_ref[...].astype(BF16), preferred_element_type=F32)
    b = jnp.dot(x, w3_ref[...].astype(BF16), preferred_element_type=F32)
    hh = (a * _sigmoid(a) * b).astype(BF16)
    o_ref[:rows, :] += jnp.dot(hh, w2_ref[...].astype(BF16), preferred_element_type=F32)


def _ffn_dense_kernel(x_ref, g_ref, w1_ref, w3_ref, w2_ref, o_ref, xb_ref):
    @pl.when(pl.program_id(1) == 0)
    def _():
        _normalise_tile(x_ref, g_ref, xb_ref, o_ref, True)

    _swiglu_rows(xb_ref, w1_ref, w3_ref, w2_ref, o_ref, ROW_TILE)


def _ffn_dense(x, g, g_row, w1, w3, w2, w_idx):
    rows = x.shape[0]
    tf = FFN_COLS
    nf = w1.shape[-1] // tf
    return pl.pallas_call(
        _ffn_dense_kernel,
        grid=(rows // ROW_TILE, nf),
        in_specs=[
            pl.BlockSpec((ROW_TILE, D_MODEL), lambda i, f: (i, 0), pipeline_mode=pl.Buffered(1)),
            pl.BlockSpec((None, 1, D_MODEL), lambda i, f: (g_row, 0, 0)),
            pl.BlockSpec((None, D_MODEL, tf), lambda i, f: (w_idx, 0, f)),
            pl.BlockSpec((None, D_MODEL, tf), lambda i, f: (w_idx, 0, f)),
            pl.BlockSpec((None, tf, D_MODEL), lambda i, f: (w_idx, f, 0)),
        ],
        out_specs=pl.BlockSpec((ROW_TILE, D_MODEL), lambda i, f: (i, 0)),
        scratch_shapes=[pltpu.VMEM((ROW_TILE, D_MODEL), BF16)],
        out_shape=jax.ShapeDtypeStruct((rows, D_MODEL), F32),
        compiler_params=_params(("parallel", "arbitrary")),
        name="ffn_dense",
    )(x, g, w1, w3, w2)


def _experts_kernel(te_ref, tv_ref, src_ref, nsrc_ref, h_hbm, g_ref, w1_ref, w3_ref, w2_ref,
                    o_ref, land_ref, xb_ref, sem, *, rows_per_step):
    i = pl.program_id(0)
    f = pl.program_id(1)
    land_rows = land_ref.shape[0]

    def row_copy(idx_ref, r):
        tok = idx_ref[jnp.minimum(r, ROW_TILE - 1)]
        return pltpu.make_async_copy(h_hbm.at[pl.ds(tok, 1)], land_ref.at[pl.ds(r, 1)], sem)

    def wait_landing():
        pltpu.make_async_copy(h_hbm.at[pl.ds(0, land_rows)], land_ref, sem).wait()

    @pl.when(jnp.logical_and(i == 0, f == 0))
    def _():
        def body(r, c):
            row_copy(src_ref, r).start()
            return c
        lax.fori_loop(0, land_rows, body, 0)
        wait_landing()

    @pl.when(f == 0)
    def _():
        _normalise_tile(land_ref, g_ref, xb_ref, o_ref, False)

    def step(groups):
        base = f * rows_per_step
        for j in range(rows_per_step):
            row_copy(nsrc_ref, base + j).start()
        if groups:
            _swiglu_rows(xb_ref, w1_ref, w3_ref, w2_ref, o_ref, groups * SUB_ROWS)

    for groups in range(ROW_TILE // SUB_ROWS + 1):
        pl.when(tv_ref[i] == groups)(functools.partial(step, groups))

    @pl.when(f == pl.num_programs(1) - 1)
    def _():
        wait_landing()


def _ffn_experts(h, g, g_row, w1, w3, w2, w_idx, tile_expert, tile_valid, src):
    n_tiles = tile_expert.shape[0]
    tf = FFN_COLS
    nf = w1.shape[-1] // tf
    rows_per_step = -(-ROW_TILE // nf)
    while (rows_per_step * nf) % 8:
        rows_per_step += 1
    land_rows = rows_per_step * nf

    def wcol(i, f, te, tv):
        return jnp.where(tv[i] > 0, f, nf - 1)

    return pl.pallas_call(
        functools.partial(_experts_kernel, rows_per_step=rows_per_step),
        grid_spec=pltpu.PrefetchScalarGridSpec(
            num_scalar_prefetch=2,
            grid=(n_tiles, nf),
            in_specs=[
                pl.BlockSpec((ROW_TILE,), lambda i, f, te, tv: (i,), memory_space=pltpu.SMEM),
                pl.BlockSpec((ROW_TILE,), lambda i, f, te, tv: (jnp.minimum(i + 1, n_tiles - 1),),
                             memory_space=pltpu.SMEM),
                pl.BlockSpec(memory_space=pl.ANY),
                pl.BlockSpec((None, 1, D_MODEL), lambda i, f, te, tv: (g_row, 0, 0)),
                pl.BlockSpec((None, None, D_MODEL, tf),
                             lambda i, f, te, tv: (w_idx, te[i], 0, wcol(i, f, te, tv))),
                pl.BlockSpec((None, None, D_MODEL, tf),
                             lambda i, f, te, tv: (w_idx, te[i], 0, wcol(i, f, te, tv))),
                pl.BlockSpec((None, None, tf, D_MODEL),
                             lambda i, f, te, tv: (w_idx, te[i], wcol(i, f, te, tv), 0)),
            ],
            out_specs=pl.BlockSpec((ROW_TILE, D_MODEL), lambda i, f, te, tv: (i, 0)),
            scratch_shapes=[pltpu.VMEM((land_rows, D_MODEL), F32),
                            pltpu.VMEM((ROW_TILE, D_MODEL), BF16),
                            pltpu.SemaphoreType.DMA(())],
        ),
        out_shape=jax.ShapeDtypeStruct((n_tiles * ROW_TILE, D_MODEL), F32),
        compiler_params=_params(("arbitrary", "arbitrary")),
        name="ffn_experts",
    )(tile_expert, tile_valid, src, src, h, g, w1, w3, w2)


def _route_kernel(h_ref, g_ref, rh_ref, rl_ref, info_ref, cnt_ref, run_ref):
    t = h_ref.shape[0]

    @pl.when(pl.program_id(0) == 0)
    def _():
        run_ref[...] = jnp.zeros_like(run_ref)

    xn = _rms_rows(h_ref[...], g_ref[...])
    xh = xn.astype(BF16)
    xl = (xn - xh.astype(F32)).astype(BF16)
    logits = (jnp.dot(xh, rh_ref[...], preferred_element_type=F32)
              + jnp.dot(xl, rh_ref[...], preferred_element_type=F32)
              + jnp.dot(xh, rl_ref[...], preferred_element_type=F32))
    lane = lax.broadcasted_iota(jnp.int32, (t, LANES), 1).astype(F32)
    neg = jnp.float32(-jnp.inf)
    logits = jnp.where(lane < N_EXPERTS, logits, neg)
    v1 = jnp.max(logits, axis=-1, keepdims=True)
    e1 = jnp.min(jnp.where(logits == v1, lane, float(LANES)), axis=-1, keepdims=True)
    rest = jnp.where(lane == e1, neg, logits)
    v2 = jnp.max(rest, axis=-1, keepdims=True)
    e2 = jnp.min(jnp.where(rest == v2, lane, float(LANES)), axis=-1, keepdims=True)
    ex = jnp.exp(v2 - v1)
    g1 = 1.0 / (1.0 + ex)
    g2 = ex / (1.0 + ex)

    onehot = jnp.where(jnp.logical_or(lane == e1, lane == e2), 1.0, 0.0)
    r_i = lax.broadcasted_iota(jnp.int32, (t, t), 0)
    c_i = lax.broadcasted_iota(jnp.int32, (t, t), 1)
    tri = jnp.where(c_i <= r_i, 1.0, 0.0).astype(BF16)
    cum = jnp.dot(tri, onehot.astype(BF16), preferred_element_type=F32)
    excl = cum - onehot + run_ref[...]
    rank1 = jnp.sum(jnp.where(lane == e1, excl, 0.0), axis=-1, keepdims=True)
    rank2 = jnp.sum(jnp.where(lane == e2, excl, 0.0), axis=-1, keepdims=True)
    run_ref[...] = run_ref[...] + cum[t - 1:t, :]

    info = jnp.where(lane == 0, e1, 0.0)
    info = jnp.where(lane == 1, e2, info)
    info = jnp.where(lane == 2, g1, info)
    info = jnp.where(lane == 3, g2, info)
    info = jnp.where(lane == 4, rank1, info)
    info = jnp.where(lane == 5, rank2, info)
    info_ref[...] = info
    cnt_ref[...] = jnp.broadcast_to(run_ref[...], cnt_ref.shape)


def _route(h, g, router, layer):
    n = h.shape[0]
    r = jnp.zeros((D_MODEL, LANES), F32).at[:, :N_EXPERTS].set(router)
    rh = r.astype(BF16)
    rl = (r - rh.astype(F32)).astype(BF16)
    t = ROUTE_TILE
    return pl.pallas_call(
        _route_kernel,
        grid=(n // t,),
        in_specs=[pl.BlockSpec((t, D_MODEL), lambda i: (i, 0)),
                  pl.BlockSpec((None, 1, D_MODEL), lambda i: (layer, 0, 0)),
                  pl.BlockSpec((D_MODEL, LANES), lambda i: (0, 0)),
                  pl.BlockSpec((D_MODEL, LANES), lambda i: (0, 0))],
        out_specs=[pl.BlockSpec((t, LANES), lambda i: (i, 0)),
                   pl.BlockSpec((8, LANES), lambda i: (0, 0))],
        out_shape=[jax.ShapeDtypeStruct((n, LANES), F32),
                   jax.ShapeDtypeStruct((8, LANES), F32)],
        scratch_shapes=[pltpu.VMEM((1, LANES), F32)],
        compiler_params=_params(("arbitrary",)),
        name="route",
    )(h, g, rh, rl)


def _combine_kernel(pos_ref, h_ref, info_ref, g_ref, ys_ref, o_ref, buf_ref, sem, *, final_norm):
    t = h_ref.shape[0]

    def row_copy(k, r, slot):
        return pltpu.make_async_copy(ys_ref.at[pl.ds(slot, 1)], buf_ref.at[k, pl.ds(r, 1)], sem)

    def start(r, c):
        row_copy(0, r, pos_ref[r]).start()
        row_copy(1, r, pos_ref[t + r]).start()
        return c
    lax.fori_loop(0, t, start, 0, unroll=8)

    for k in range(TOP_K):
        pltpu.make_async_copy(ys_ref.at[pl.ds(0, t)], buf_ref.at[k], sem).wait()

    info = info_ref[...]
    lane = lax.broadcasted_iota(jnp.int32, info.shape, 1)
    g1 = jnp.sum(jnp.where(lane == 2, info, 0.0), axis=-1, keepdims=True)
    g2 = jnp.sum(jnp.where(lane == 3, info, 0.0), axis=-1, keepdims=True)
    y = h_ref[...] + g1 * buf_ref[0] + g2 * buf_ref[1]
    o_ref[...] = _rms_rows(y, g_ref[...]) if final_norm else y


def _combine(h, info, norm_final, ys, pos_tiles, final_norm):
    n = h.shape[0]
    t = COMBINE_TILE
    return pl.pallas_call(
        functools.partial(_combine_kernel, final_norm=final_norm),
        grid=(n // t,),
        in_specs=[pl.BlockSpec((2 * t,), lambda i: (i,), memory_space=pltpu.SMEM),
                  pl.BlockSpec((t, D_MODEL), lambda i: (i, 0)),
                  pl.BlockSpec((t, LANES), lambda i: (i, 0)),
                  pl.BlockSpec((1, D_MODEL), lambda i: (0, 0)),
                  pl.BlockSpec(memory_space=pl.ANY)],
        out_specs=pl.BlockSpec((t, D_MODEL), lambda i: (i, 0)),
        out_shape=jax.ShapeDtypeStruct((n, D_MODEL), F32),
        scratch_shapes=[pltpu.VMEM((2, t, D_MODEL), F32), pltpu.SemaphoreType.DMA(())],
        compiler_params=_params(("arbitrary",)),
        name="combine",
    )(pos_tiles, h, info, norm_final.reshape(1, D_MODEL), ys)


def _pos_tiles(pos1, pos2, tile):
    nt = pos1.shape[0] // tile
    return jnp.stack([pos1.reshape(nt, tile), pos2.reshape(nt, tile)], axis=1).reshape(-1)


def _moe_layer(h, norm_g, layer, router, w1, w3, w2, moe_idx, norm_final, final_norm):
    n = h.shape[0]
    info, cnt = _route(h, norm_g, router, layer)
    counts = cnt[0, :N_EXPERTS].astype(jnp.int32)
    padded = ((counts + ROW_TILE - 1) // ROW_TILE) * ROW_TILE
    ends = jnp.cumsum(padded)
    starts = ends - padded
    e1 = info[:, 0].astype(jnp.int32)
    e2 = info[:, 1].astype(jnp.int32)
    pos1 = starts[e1] + info[:, 4].astype(jnp.int32)
    pos2 = starts[e2] + info[:, 5].astype(jnp.int32)

    n_tiles = (n * TOP_K) // ROW_TILE + N_EXPERTS
    tile_start = jnp.arange(n_tiles, dtype=jnp.int32) * ROW_TILE
    in_use = tile_start < ends[-1]
    tile_expert = jnp.sum((tile_start[:, None] >= ends[None, :]).astype(jnp.int32), axis=1)
    last_expert = jnp.sum((ends[-1] - 1 >= ends).astype(jnp.int32))
    tile_expert = jnp.where(in_use, tile_expert, last_expert).astype(jnp.int32)
    tile_rows = jnp.clip((starts + counts)[tile_expert] - tile_start, 0, ROW_TILE)
    tile_valid = jnp.where(in_use, (tile_rows + SUB_ROWS - 1) // SUB_ROWS, 0).astype(jnp.int32)

    tok = jnp.arange(n, dtype=jnp.int32)
    src = jnp.zeros((n_tiles * ROW_TILE,), jnp.int32).at[jnp.concatenate([pos1, pos2])].set(
        jnp.concatenate([tok, tok]), unique_indices=True)

    ys = _ffn_experts(h, norm_g, layer, w1, w3, w2, moe_idx, tile_expert, tile_valid, src)
    return _combine(h, info, norm_final, ys, _pos_tiles(pos1, pos2, COMBINE_TILE), final_norm)


def kernel(x, positions, norm_mix, norm_ffn, norm_final, w_in, conv_w, w_ret_out, w_conv_out,
           w_out, ffn_w1, ffn_w3, ffn_w2, router, moe_w1, moe_w3, moe_w2):
    bsz, seq, _ = x.shape
    depth = w_in.shape[0]
    n = bsz * seq
    h = x.reshape(n, D_MODEL)
    norm_mix = norm_mix.reshape(depth, 1, D_MODEL)
    norm_ffn = norm_ffn.reshape(depth, 1, D_MODEL)
    cos, sin = _rope_tables(positions)
    consts = _retention_consts()
    w_ret_out, w_conv_out, w_out = (w.astype(BF16) for w in (w_ret_out, w_conv_out, w_out))
    for layer in range(depth):
        proj = _inproj(_norm_rows(h, norm_mix, layer, BF16), w_in, layer)
        og = _retention(proj, cos, sin, consts, bsz, seq)
        h = _mix_out(og, proj, h, conv_w, w_ret_out, w_conv_out, w_out, layer, seq)
        i = layer // 2
        last = layer == depth - 1
        if layer % 2 == 0:
            h = _ffn_dense(h, norm_ffn, layer, ffn_w1, ffn_w3, ffn_w2, i)
            if last:
                h = _norm_rows(h, norm_final.reshape(1, 1, D_MODEL), 0, F32)
        else:
            h = _moe_layer(h, norm_ffn, layer, router[i], moe_w1, moe_w3, moe_w2, i,
                           norm_final, last)
    return h.reshape(bsz, seq, D_MODEL)
```

```python
import functools

import jax
import jax.numpy as jnp
from jax import lax
from jax.experimental import pallas as pl
from jax.experimental.pallas import tpu as pltpu

F32 = jnp.float32
BF16 = jnp.bfloat16

D_MODEL = 2048
RET_HEADS = 4
RET_QK_DIM = 256
RET_V_DIM = 512
RET_BLOCK = 256
ROPE_BASE = 10000.0
CONV_K = 3
N_EXPERTS = 8
TOP_K = 2
EPS = 1e-6
HALF = RET_QK_DIM // 2

OFF_Q, OFF_K, OFF_V, OFF_G = 0, 1024, 2048, 4096
OFF_CB, OFF_CC, OFF_CH, OFF_GA, OFF_GB = 6144, 8192, 10240, 12288, 14336
D_IN_TOTAL = 16384

LANES = 128
VMEM_LIMIT = 56 * 1024 * 1024

ROW_TILE = 1024
INPROJ_ROWS = 1024
SUB_ROWS = 256
MIX_TILE = 256
FFN_COLS = 512
FFN_VMEM_LIMIT = 61 * 1024 * 1024
ROUTE_TILE = 512
CONV_LANES = 512
COMBINE_TILE = 256


def _params(sem, vmem=VMEM_LIMIT):
    return pltpu.CompilerParams(dimension_semantics=sem, vmem_limit_bytes=vmem)


def _rms_rows(x, g):
    ms = jnp.mean(x * x, axis=-1, keepdims=True)
    return x * lax.rsqrt(ms + EPS) * g


def _sigmoid(x):
    return 1.0 / (1.0 + jnp.exp(-x))


def _rope_kernel(pos_ref, invf_ref, cos_ref, sin_ref):
    ang = pos_ref[...].astype(F32) * invf_ref[...]
    cos_ref[...] = jnp.cos(ang)
    sin_ref[...] = jnp.sin(ang)


def _rope_tables(positions):
    n = positions.size
    inv_freq = ROPE_BASE ** (-jnp.arange(HALF, dtype=F32) / HALF)
    rows = 2048
    return pl.pallas_call(
        _rope_kernel,
        grid=(n // rows,),
        in_specs=[pl.BlockSpec((rows, 1), lambda i: (i, 0)),
                  pl.BlockSpec((1, HALF), lambda i: (0, 0))],
        out_specs=[pl.BlockSpec((rows, HALF), lambda i: (i, 0)),
                   pl.BlockSpec((rows, HALF), lambda i: (i, 0))],
        out_shape=[jax.ShapeDtypeStruct((n, HALF), F32)] * 2,
        compiler_params=_params(("parallel",)),
        name="rope_tables",
    )(positions.reshape(n, 1), inv_freq.reshape(1, HALF))


def _norm_kernel(h_ref, g_ref, o_ref):
    o_ref[...] = _rms_rows(h_ref[...], g_ref[...]).astype(o_ref.dtype)


def _norm_rows(h, g, row, dtype):
    n = h.shape[0]
    t = 512
    return pl.pallas_call(
        _norm_kernel,
        grid=(n // t,),
        in_specs=[pl.BlockSpec((t, D_MODEL), lambda i: (i, 0)),
                  pl.BlockSpec((None, 1, D_MODEL), lambda i: (row, 0, 0))],
        out_specs=pl.BlockSpec((t, D_MODEL), lambda i: (i, 0)),
        out_shape=jax.ShapeDtypeStruct((n, D_MODEL), dtype),
        compiler_params=_params(("parallel",)),
        name="norm_rows",
    )(h, g)


def _inproj_kernel(x_ref, w_ref, o_ref, wb_ref):
    @pl.when(pl.program_id(1) == 0)
    def _():
        wb_ref[...] = w_ref[...].astype(BF16)

    o_ref[...] = jnp.dot(x_ref[...], wb_ref[...],
                         preferred_element_type=F32).astype(o_ref.dtype)


def _inproj(xn, w_in, layer):
    n = xn.shape[0]
    tn = 1024
    tm = INPROJ_ROWS
    return pl.pallas_call(
        _inproj_kernel,
        grid=(D_IN_TOTAL // tn, n // tm),
        in_specs=[pl.BlockSpec((tm, D_MODEL), lambda j, i: (i, 0)),
                  pl.BlockSpec((None, D_MODEL, tn), lambda j, i: (layer, 0, j))],
        out_specs=pl.BlockSpec((tm, tn), lambda j, i: (i, j)),
        out_shape=jax.ShapeDtypeStruct((n, D_IN_TOTAL), BF16),
        scratch_shapes=[pltpu.VMEM((D_MODEL, tn), BF16)],
        compiler_params=_params(("parallel", "arbitrary")),
        name="inproj",
    )(xn, w_in)


def _retention_kernel(gc_ref, q_ref, k_ref, v_ref, g_ref, cos_ref, sin_ref, dm_ref, xi_ref,
                      zeta_ref, o_ref, qr_ref, qx_ref, kr_ref, kz_ref, st_ref):
    seq = q_ref.shape[0]
    scale = RET_QK_DIM ** -0.5

    def rot(r, c):
        sl = pl.ds(pl.multiple_of(r * RET_BLOCK, RET_BLOCK), RET_BLOCK)
        cos = cos_ref[sl, :]
        sin = sin_ref[sl, :]
        xi = xi_ref[...]
        zeta = zeta_ref[...]
        q = q_ref[sl, :].astype(F32)
        q1, q2 = q[:, :HALF], q[:, HALF:]
        qa = (q1 * cos - q2 * sin) * scale
        qb = (q1 * sin + q2 * cos) * scale
        qr_ref[sl, :HALF] = qa.astype(BF16)
        qr_ref[sl, HALF:] = qb.astype(BF16)
        qx_ref[sl, :HALF] = (qa * xi).astype(BF16)
        qx_ref[sl, HALF:] = (qb * xi).astype(BF16)
        k = k_ref[sl, :].astype(F32)
        k1, k2 = k[:, :HALF], k[:, HALF:]
        ka = k1 * cos - k2 * sin
        kb = k1 * sin + k2 * cos
        kr_ref[sl, :HALF] = ka.astype(BF16)
        kr_ref[sl, HALF:] = kb.astype(BF16)
        kz_ref[sl, :HALF] = (ka * zeta).astype(BF16)
        kz_ref[sl, HALF:] = (kb * zeta).astype(BF16)
        return c
    lax.fori_loop(0, seq // RET_BLOCK, rot, 0)

    st_ref[...] = jnp.zeros_like(st_ref)
    gamma_block = gc_ref[pl.program_id(1)]

    def block(c, carry):
        sl = pl.ds(pl.multiple_of(c * RET_BLOCK, RET_BLOCK), RET_BLOCK)
        vc = v_ref[sl, :]
        scores = lax.dot_general(qr_ref[sl, :], kr_ref[sl, :], (((1,), (1,)), ((), ())),
                                 preferred_element_type=F32) * dm_ref[...]
        inner = jnp.dot(scores.astype(BF16), vc, preferred_element_type=F32)
        state = st_ref[...]
        cross = jnp.dot(qx_ref[sl, :], state.astype(BF16), preferred_element_type=F32)
        st_ref[...] = state * gamma_block + lax.dot_general(
            kz_ref[sl, :], vc, (((0,), (0,)), ((), ())), preferred_element_type=F32)
        o = inner + cross
        mu = jnp.mean(o, axis=-1, keepdims=True)
        d = o - mu
        var = jnp.mean(d * d, axis=-1, keepdims=True)
        on = d * lax.rsqrt(var + EPS)
        g = g_ref[sl, :].astype(F32)
        o_ref[sl, :] = (on * (g * _sigmoid(g))).astype(o_ref.dtype)
        return carry
    lax.fori_loop(0, seq // RET_BLOCK, block, 0, unroll=4)


def _retention_consts():
    h = RET_HEADS
    log_gamma = jnp.log1p(-jnp.power(2.0, -5.0 - jnp.arange(h, dtype=F32)))
    idx = jnp.arange(RET_BLOCK, dtype=F32)
    diff = idx[:, None] - idx[None, :]
    causal = diff >= 0
    decay_mask = jnp.where(causal[None],
                           jnp.exp(jnp.where(causal, diff, 0.0)[None] * log_gamma[:, None, None]),
                           0.0)
    xi = jnp.exp((idx + 1.0)[None] * log_gamma[:, None])
    zeta = jnp.exp((RET_BLOCK - 1.0 - idx)[None] * log_gamma[:, None])
    gamma_block = jnp.exp(RET_BLOCK * log_gamma)
    xi_b = jnp.broadcast_to(xi[:, :, None], (h, RET_BLOCK, HALF))
    zeta_b = jnp.broadcast_to(zeta[:, :, None], (h, RET_BLOCK, HALF))
    return decay_mask, xi_b, zeta_b, gamma_block


def _retention(proj, cos, sin, consts, bsz, seq):
    decay_mask, xi_b, zeta_b, gamma_chunk = consts
    n = proj.shape[0]
    qb, vb = RET_QK_DIM, RET_V_DIM
    return pl.pallas_call(
        _retention_kernel,
        grid=(bsz, RET_HEADS),
        in_specs=[
            pl.BlockSpec(memory_space=pltpu.SMEM),
            pl.BlockSpec((seq, qb), lambda b, h: (b, OFF_Q // qb + h)),
            pl.BlockSpec((seq, qb), lambda b, h: (b, OFF_K // qb + h)),
            pl.BlockSpec((seq, vb), lambda b, h: (b, OFF_V // vb + h)),
            pl.BlockSpec((seq, vb), lambda b, h: (b, OFF_G // vb + h)),
            pl.BlockSpec((seq, HALF), lambda b, h: (b, 0)),
            pl.BlockSpec((seq, HALF), lambda b, h: (b, 0)),
            pl.BlockSpec((None, RET_BLOCK, RET_BLOCK), lambda b, h: (h, 0, 0)),
            pl.BlockSpec((None, RET_BLOCK, HALF), lambda b, h: (h, 0, 0)),
            pl.BlockSpec((None, RET_BLOCK, HALF), lambda b, h: (h, 0, 0)),
        ],
        out_specs=pl.BlockSpec((seq, vb), lambda b, h: (b, h)),
        scratch_shapes=[pltpu.VMEM((seq, qb), BF16), pltpu.VMEM((seq, qb), BF16),
                        pltpu.VMEM((seq, qb), BF16), pltpu.VMEM((seq, qb), BF16),
                        pltpu.VMEM((qb, vb), F32)],
        out_shape=jax.ShapeDtypeStruct((n, RET_HEADS * vb), BF16),
        compiler_params=_params(("parallel", "parallel")),
        name="retention",
    )(gamma_chunk, proj, proj, proj, proj, cos, sin, decay_mask, xi_b, zeta_b)


def _mix_out_kernel(og_ref, cb_ref, cc_ref, ch_ref, ga_ref, gb_ref, h_ref, cw_ref,
                    wr_ref, wc_ref, wo_ref, o_ref, tail_ref, *, tiles_per_seq):
    t = og_ref.shape[0]

    @pl.when(pl.program_id(0) % tiles_per_seq == 0)
    def _():
        tail_ref[...] = jnp.zeros_like(tail_ref)

    yr = jnp.dot(og_ref[...], wr_ref[...], preferred_element_type=F32)
    row = lax.broadcasted_iota(jnp.int32, (t, CONV_LANES), 0)
    yc = None
    for c in range(D_MODEL // CONV_LANES):
        cs = slice(c * CONV_LANES, (c + 1) * CONV_LANES)
        u = cc_ref[:, cs].astype(F32) * ch_ref[:, cs].astype(F32)
        prev1 = tail_ref[7:8, cs]
        prev2 = tail_ref[6:7, cs]
        u1 = jnp.where(row >= 1, pltpu.roll(u, 1, 0), prev1)
        u2 = jnp.where(row >= 2, pltpu.roll(u, 2, 0), jnp.where(row == 1, prev1, prev2))
        w = cw_ref[:, cs]
        y = w[0:1, :] * u2 + w[1:2, :] * u1 + w[2:3, :] * u
        cv = (cb_ref[:, cs].astype(F32) * y).astype(BF16)
        tail_ref[:, cs] = u[t - 8:t, :]
        part = jnp.dot(cv, wc_ref[cs, :], preferred_element_type=F32)
        yc = part if yc is None else yc + part
    ga = ga_ref[...].astype(F32)
    gb = gb_ref[...].astype(F32)
    merged = (_sigmoid(ga) * yr + _sigmoid(gb) * yc).astype(BF16)
    o_ref[...] = h_ref[...] + jnp.dot(merged, wo_ref[...], preferred_element_type=F32)


def _mix_out(og, proj, h, conv_w, w_ret_out, w_conv_out, w_out, layer, seq):
    n = h.shape[0]
    t = MIX_TILE
    rows = lambda i: (i, 0)
    pcol = lambda off: pl.BlockSpec((t, D_MODEL), lambda i: (i, off // D_MODEL))
    wspec = pl.BlockSpec((None, D_MODEL, D_MODEL), lambda i: (layer, 0, 0),
                         pipeline_mode=pl.Buffered(1))
    return pl.pallas_call(
        functools.partial(_mix_out_kernel, tiles_per_seq=seq // t),
        grid=(n // t,),
        in_specs=[pl.BlockSpec((t, D_MODEL), rows),
                  pcol(OFF_CB), pcol(OFF_CC), pcol(OFF_CH), pcol(OFF_GA), pcol(OFF_GB),
                  pl.BlockSpec((t, D_MODEL), rows),
                  pl.BlockSpec((None, CONV_K, D_MODEL), lambda i: (layer, 0, 0)),
                  wspec, wspec, wspec],
        out_specs=pl.BlockSpec((t, D_MODEL), rows),
        out_shape=jax.ShapeDtypeStruct((n, D_MODEL), F32),
        scratch_shapes=[pltpu.VMEM((8, D_MODEL), F32)],
        compiler_params=_params(("arbitrary",)),
        name="mix_out",
    )(og, proj, proj, proj, proj, proj, h, conv_w, w_ret_out, w_conv_out, w_out)


def _normalise_tile(src_ref, g_ref, xb_ref, o_ref, residual):
    def body(r, c):
        sl = pl.ds(pl.multiple_of(r * SUB_ROWS, SUB_ROWS), SUB_ROWS)
        x = src_ref[sl, :]
        xb_ref[sl, :] = _rms_rows(x, g_ref[...]).astype(BF16)
        o_ref[sl, :] = x if residual else jnp.zeros_like(x)
        return c
    lax.fori_loop(0, ROW_TILE // SUB_ROWS, body, 0)


def _swiglu_rows(xb_ref, w1_ref, w3_ref, w2_ref, o_ref, rows):
    x = xb_ref[:rows, :]
    a = jnp.dot(x, w1_ref[...].astype(BF16), preferred_element_type=F32)
    b = jnp.dot(x, w3_ref[...].astype(BF16), preferred_element_type=F32)
    hh = (a * _sigmoid(a) * b).astype(BF16)
    o_ref[:rows, :] += jnp.dot(hh, w2_ref[...].astype(BF16), preferred_element_type=F32)


def _ffn_dense_kernel(x_ref, g_ref, w1_ref, w3_ref, w2_ref, o_ref, xb_ref):
    @pl.when(pl.program_id(1) == 0)
    def _():
        _normalise_tile(x_ref, g_ref, xb_ref, o_ref, True)

    _swiglu_rows(xb_ref, w1_ref, w3_ref, w2_ref, o_ref, ROW_TILE)


def _ffn_dense(x, g, g_row, w1, w3, w2, w_idx):
    rows = x.shape[0]
    tf = FFN_COLS
    nf = w1.shape[-1] // tf
    return pl.pallas_call(
        _ffn_dense_kernel,
        grid=(rows // ROW_TILE, nf),
        in_specs=[
            pl.BlockSpec((ROW_TILE, D_MODEL), lambda i, f: (i, 0), pipeline_mode=pl.Buffered(1)),
            pl.BlockSpec((None, 1, D_MODEL), lambda i, f: (g_row, 0, 0)),
            pl.BlockSpec((None, D_MODEL, tf), lambda i, f: (w_idx, 0, f)),
            pl.BlockSpec((None, D_MODEL, tf), lambda i, f: (w_idx, 0, f)),
            pl.BlockSpec((None, tf, D_MODEL), lambda i, f: (w_idx, f, 0)),
        ],
        out_specs=pl.BlockSpec((ROW_TILE, D_MODEL), lambda i, f: (i, 0)),
        scratch_shapes=[pltpu.VMEM((ROW_TILE, D_MODEL), BF16)],
        out_shape=jax.ShapeDtypeStruct((rows, D_MODEL), F32),
        compiler_params=_params(("parallel", "arbitrary"), FFN_VMEM_LIMIT),
        name="ffn_dense",
    )(x, g, w1, w3, w2)


def _experts_kernel(te_ref, tv_ref, src_ref, nsrc_ref, h_hbm, g_ref, w1_ref, w3_ref, w2_ref,
                    o_ref, land_ref, xb_ref, sem, *, rows_per_step):
    i = pl.program_id(0)
    f = pl.program_id(1)
    land_rows = land_ref.shape[0]

    def row_copy(idx_ref, r):
        tok = idx_ref[jnp.minimum(r, ROW_TILE - 1)]
        return pltpu.make_async_copy(h_hbm.at[pl.ds(tok, 1)], land_ref.at[pl.ds(r, 1)], sem)

    def wait_landing():
        pltpu.make_async_copy(h_hbm.at[pl.ds(0, land_rows)], land_ref, sem).wait()

    @pl.when(jnp.logical_and(i == 0, f == 0))
    def _():
        def body(r, c):
            row_copy(src_ref, r).start()
            return c
        lax.fori_loop(0, land_rows, body, 0)
        wait_landing()

    @pl.when(f == 0)
    def _():
        _normalise_tile(land_ref, g_ref, xb_ref, o_ref, False)

    def step(groups):
        base = f * rows_per_step
        for j in range(rows_per_step):
            row_copy(nsrc_ref, base + j).start()
        if groups:
            _swiglu_rows(xb_ref, w1_ref, w3_ref, w2_ref, o_ref, groups * SUB_ROWS)

    for groups in range(ROW_TILE // SUB_ROWS + 1):
        pl.when(tv_ref[i] == groups)(functools.partial(step, groups))

    @pl.when(f == pl.num_programs(1) - 1)
    def _():
        wait_landing()


def _ffn_experts(h, g, g_row, w1, w3, w2, w_idx, tile_expert, tile_valid, src):
    n_tiles = tile_expert.shape[0]
    tf = FFN_COLS
    nf = w1.shape[-1] // tf
    rows_per_step = -(-ROW_TILE // nf)
    while (rows_per_step * nf) % 8:
        rows_per_step += 1
    land_rows = rows_per_step * nf

    def wcol(i, f, te, tv):
        return jnp.where(tv[i] > 0, f, nf - 1)

    return pl.pallas_call(
        functools.partial(_experts_kernel, rows_per_step=rows_per_step),
        grid_spec=pltpu.PrefetchScalarGridSpec(
            num_scalar_prefetch=2,
            grid=(n_tiles, nf),
            in_specs=[
                pl.BlockSpec((ROW_TILE,), lambda i, f, te, tv: (i,), memory_space=pltpu.SMEM),
                pl.BlockSpec((ROW_TILE,), lambda i, f, te, tv: (jnp.minimum(i + 1, n_tiles - 1),),
                             memory_space=pltpu.SMEM),
                pl.BlockSpec(memory_space=pl.ANY),
                pl.BlockSpec((None, 1, D_MODEL), lambda i, f, te, tv: (g_row, 0, 0)),
                pl.BlockSpec((None, None, D_MODEL, tf),
                             lambda i, f, te, tv: (w_idx, te[i], 0, wcol(i, f, te, tv))),
                pl.BlockSpec((None, None, D_MODEL, tf),
                             lambda i, f, te, tv: (w_idx, te[i], 0, wcol(i, f, te, tv))),
                pl.BlockSpec((None, None, tf, D_MODEL),
                             lambda i, f, te, tv: (w_idx, te[i], wcol(i, f, te, tv), 0)),
            ],
            out_specs=pl.BlockSpec((ROW_TILE, D_MODEL), lambda i, f, te, tv: (i, 0)),
            scratch_shapes=[pltpu.VMEM((land_rows, D_MODEL), F32),
                            pltpu.VMEM((ROW_TILE, D_MODEL), BF16),
                            pltpu.SemaphoreType.DMA(())],
        ),
        out_shape=jax.ShapeDtypeStruct((n_tiles * ROW_TILE, D_MODEL), F32),
        compiler_params=_params(("arbitrary", "arbitrary"), FFN_VMEM_LIMIT),
        name="ffn_experts",
    )(tile_expert, tile_valid, src, src, h, g, w1, w3, w2)


def _route_kernel(h_ref, g_ref, rh_ref, rl_ref, info_ref, cnt_ref, run_ref):
    t = h_ref.shape[0]

    @pl.when(pl.program_id(0) == 0)
    def _():
        run_ref[...] = jnp.zeros_like(run_ref)

    xn = _rms_rows(h_ref[...], g_ref[...])
    xh = xn.astype(BF16)
    xl = (xn - xh.astype(F32)).astype(BF16)
    logits = (jnp.dot(xh, rh_ref[...], preferred_element_type=F32)
              + jnp.dot(xl, rh_ref[...], preferred_element_type=F32)
              + jnp.dot(xh, rl_ref[...], preferred_element_type=F32))
    lane = lax.broadcasted_iota(jnp.int32, (t, LANES), 1).astype(F32)
    neg = jnp.float32(-jnp.inf)
    logits = jnp.where(lane < N_EXPERTS, logits, neg)
    v1 = jnp.max(logits, axis=-1, keepdims=True)
    e1 = jnp.min(jnp.where(logits == v1, lane, float(LANES)), axis=-1, keepdims=True)
    rest = jnp.where(lane == e1, neg, logits)
    v2 = jnp.max(rest, axis=-1, keepdims=True)
    e2 = jnp.min(jnp.where(rest == v2, lane, float(LANES)), axis=-1, keepdims=True)
    ex = jnp.exp(v2 - v1)
    g1 = 1.0 / (1.0 + ex)
    g2 = ex / (1.0 + ex)

    onehot = jnp.where(jnp.logical_or(lane == e1, lane == e2), 1.0, 0.0)
    r_i = lax.broadcasted_iota(jnp.int32, (t, t), 0)
    c_i = lax.broadcasted_iota(jnp.int32, (t, t), 1)
    tri = jnp.where(c_i <= r_i, 1.0, 0.0).astype(BF16)
    cum = jnp.dot(tri, onehot.astype(BF16), preferred_element_type=F32)
    excl = cum - onehot + run_ref[...]
    rank1 = jnp.sum(jnp.where(lane == e1, excl, 0.0), axis=-1, keepdims=True)
    rank2 = jnp.sum(jnp.where(lane == e2, excl, 0.0), axis=-1, keepdims=True)
    run_ref[...] = run_ref[...] + cum[t - 1:t, :]

    info = jnp.where(lane == 0, e1, 0.0)
    info = jnp.where(lane == 1, e2, info)
    info = jnp.where(lane == 2, g1, info)
    info = jnp.where(lane == 3, g2, info)
    info = jnp.where(lane == 4, rank1, info)
    info = jnp.where(lane == 5, rank2, info)
    info_ref[...] = info
    cnt_ref[...] = jnp.broadcast_to(run_ref[...], cnt_ref.shape)


def _route(h, g, router, layer):
    n = h.shape[0]
    r = jnp.zeros((D_MODEL, LANES), F32).at[:, :N_EXPERTS].set(router)
    rh = r.astype(BF16)
    rl = (r - rh.astype(F32)).astype(BF16)
    t = ROUTE_TILE
    return pl.pallas_call(
        _route_kernel,
        grid=(n // t,),
        in_specs=[pl.BlockSpec((t, D_MODEL), lambda i: (i, 0)),
                  pl.BlockSpec((None, 1, D_MODEL), lambda i: (layer, 0, 0)),
                  pl.BlockSpec((D_MODEL, LANES), lambda i: (0, 0)),
                  pl.BlockSpec((D_MODEL, LANES), lambda i: (0, 0))],
        out_specs=[pl.BlockSpec((t, LANES), lambda i: (i, 0)),
                   pl.BlockSpec((8, LANES), lambda i: (0, 0))],
        out_shape=[jax.ShapeDtypeStruct((n, LANES), F32),
                   jax.ShapeDtypeStruct((8, LANES), F32)],
        scratch_shapes=[pltpu.VMEM((1, LANES), F32)],
        compiler_params=_params(("arbitrary",)),
        name="route",
    )(h, g, rh, rl)


def _combine_kernel(pos_ref, npos_ref, h_ref, info_ref, g_ref, ys_ref, o_ref, buf_ref, sem, *,
                    final_norm):
    t = h_ref.shape[0]
    i = pl.program_id(0)
    cur = i % 2

    def gather(idx_ref, half):
        def start(r, c):
            for k in range(TOP_K):
                pltpu.make_async_copy(ys_ref.at[pl.ds(idx_ref[k * t + r], 1)],
                                      buf_ref.at[half, k, pl.ds(r, 1)], sem.at[half]).start()
            return c
        lax.fori_loop(0, t, start, 0, unroll=8)

    @pl.when(i == 0)
    def _():
        gather(pos_ref, 0)

    @pl.when(i + 1 < pl.num_programs(0))
    def _():
        gather(npos_ref, 1 - cur)

    for k in range(TOP_K):
        pltpu.make_async_copy(ys_ref.at[pl.ds(0, t)], buf_ref.at[cur, k], sem.at[cur]).wait()

    info = info_ref[...]
    lane = lax.broadcasted_iota(jnp.int32, info.shape, 1)
    g1 = jnp.sum(jnp.where(lane == 2, info, 0.0), axis=-1, keepdims=True)
    g2 = jnp.sum(jnp.where(lane == 3, info, 0.0), axis=-1, keepdims=True)
    y = h_ref[...] + g1 * buf_ref[cur, 0] + g2 * buf_ref[cur, 1]
    o_ref[...] = _rms_rows(y, g_ref[...]) if final_norm else y


def _combine(h, info, norm_final, ys, pos_tiles, final_norm):
    n = h.shape[0]
    t = COMBINE_TILE
    last = n // t - 1
    return pl.pallas_call(
        functools.partial(_combine_kernel, final_norm=final_norm),
        grid=(n // t,),
        in_specs=[pl.BlockSpec((TOP_K * t,), lambda i: (i,), memory_space=pltpu.SMEM),
                  pl.BlockSpec((TOP_K * t,), lambda i: (jnp.minimum(i + 1, last),),
                               memory_space=pltpu.SMEM),
                  pl.BlockSpec((t, D_MODEL), lambda i: (i, 0)),
                  pl.BlockSpec((t, LANES), lambda i: (i, 0)),
                  pl.BlockSpec((1, D_MODEL), lambda i: (0, 0)),
                  pl.BlockSpec(memory_space=pl.ANY)],
        out_specs=pl.BlockSpec((t, D_MODEL), lambda i: (i, 0)),
        out_shape=jax.ShapeDtypeStruct((n, D_MODEL), F32),
        scratch_shapes=[pltpu.VMEM((2, TOP_K, t, D_MODEL), F32), pltpu.SemaphoreType.DMA((2,))],
        compiler_params=_params(("arbitrary",)),
        name="combine",
    )(pos_tiles, pos_tiles, h, info, norm_final.reshape(1, D_MODEL), ys)


def _pos_tiles(pos1, pos2, tile):
    nt = pos1.shape[0] // tile
    return jnp.stack([pos1.reshape(nt, tile), pos2.reshape(nt, tile)], axis=1).reshape(-1)


def _moe_layer(h, norm_g, layer, router, w1, w3, w2, moe_idx, norm_final, final_norm):
    n = h.shape[0]
    info, cnt = _route(h, norm_g, router, layer)
    counts = cnt[0, :N_EXPERTS].astype(jnp.int32)
    padded = ((counts + ROW_TILE - 1) // ROW_TILE) * ROW_TILE
    ends = jnp.cumsum(padded)
    starts = ends - padded
    e1 = info[:, 0].astype(jnp.int32)
    e2 = info[:, 1].astype(jnp.int32)
    pos1 = starts[e1] + info[:, 4].astype(jnp.int32)
    pos2 = starts[e2] + info[:, 5].astype(jnp.int32)

    n_tiles = (n * TOP_K) // ROW_TILE + N_EXPERTS
    tile_start = jnp.arange(n_tiles, dtype=jnp.int32) * ROW_TILE
    in_use = tile_start < ends[-1]
    tile_expert = jnp.sum((tile_start[:, None] >= ends[None, :]).astype(jnp.int32), axis=1)
    last_expert = jnp.sum((ends[-1] - 1 >= ends).astype(jnp.int32))
    tile_expert = jnp.where(in_use, tile_expert, last_expert).astype(jnp.int32)
    tile_rows = jnp.clip((starts + counts)[tile_expert] - tile_start, 0, ROW_TILE)
    tile_valid = jnp.where(in_use, (tile_rows + SUB_ROWS - 1) // SUB_ROWS, 0).astype(jnp.int32)

    tok = jnp.arange(n, dtype=jnp.int32)
    src = jnp.zeros((n_tiles * ROW_TILE,), jnp.int32).at[jnp.concatenate([pos1, pos2])].set(
        jnp.concatenate([tok, tok]), unique_indices=True)

    ys = _ffn_experts(h, norm_g, layer, w1, w3, w2, moe_idx, tile_expert, tile_valid, src)
    return _combine(h, info, norm_final, ys, _pos_tiles(pos1, pos2, COMBINE_TILE), final_norm)


def kernel(x, positions, norm_mix, norm_ffn, norm_final, w_in, conv_w, w_ret_out, w_conv_out,
           w_out, ffn_w1, ffn_w3, ffn_w2, router, moe_w1, moe_w3, moe_w2):
    bsz, seq, _ = x.shape
    depth = w_in.shape[0]
    n = bsz * seq
    h = x.reshape(n, D_MODEL)
    norm_mix = norm_mix.reshape(depth, 1, D_MODEL)
    norm_ffn = norm_ffn.reshape(depth, 1, D_MODEL)
    cos, sin = _rope_tables(positions)
    consts = _retention_consts()
    w_ret_out, w_conv_out, w_out = (w.astype(BF16) for w in (w_ret_out, w_conv_out, w_out))
    for layer in range(depth):
        proj = _inproj(_norm_rows(h, norm_mix, layer, BF16), w_in, layer)
        og = _retention(proj, cos, sin, consts, bsz, seq)
        h = _mix_out(og, proj, h, conv_w, w_ret_out, w_conv_out, w_out, layer, seq)
        i = layer // 2
        last = layer == depth - 1
        if layer % 2 == 0:
            h = _ffn_dense(h, norm_ffn, layer, ffn_w1, ffn_w3, ffn_w2, i)
            if last:
                h = _norm_rows(h, norm_final.reshape(1, 1, D_MODEL), 0, F32)
        else:
            h = _moe_layer(h, norm_ffn, layer, router[i], moe_w1, moe_w3, moe_w2, i,
                           norm_final, last)
    return h.reshape(bsz, seq, D_MODEL)
```

```python
import functools

import jax
import jax.numpy as jnp
from jax import lax
from jax.experimental import pallas as pl
from jax.experimental.pallas import tpu as pltpu

F32 = jnp.float32
BF16 = jnp.bfloat16

D_MODEL = 2048
RET_HEADS = 4
RET_QK_DIM = 256
RET_V_DIM = 512
RET_BLOCK = 256
ROPE_BASE = 10000.0
CONV_K = 3
N_EXPERTS = 8
TOP_K = 2
EPS = 1e-6
HALF = RET_QK_DIM // 2

OFF_Q, OFF_K, OFF_V, OFF_G = 0, 1024, 2048, 4096
OFF_CB, OFF_CC, OFF_CH, OFF_GA, OFF_GB = 6144, 8192, 10240, 12288, 14336
D_IN_TOTAL = 16384

LANES = 128
VMEM_LIMIT = 56 * 1024 * 1024

ROW_TILE = 1024
INPROJ_ROWS = 2048
SUB_ROWS = 256
MIX_TILE = 256
FFN_COLS = 512
FFN_VMEM_LIMIT = 61 * 1024 * 1024
ROUTE_TILE = 512
CONV_LANES = 512
COMBINE_TILE = 256


def _params(sem, vmem=VMEM_LIMIT):
    return pltpu.CompilerParams(dimension_semantics=sem, vmem_limit_bytes=vmem)


def _rms_rows(x, g):
    ms = jnp.mean(x * x, axis=-1, keepdims=True)
    return x * lax.rsqrt(ms + EPS) * g


def _sigmoid(x):
    return 1.0 / (1.0 + jnp.exp(-x))


def _rope_kernel(pos_ref, invf_ref, cos_ref, sin_ref):
    ang = pos_ref[...].astype(F32) * invf_ref[...]
    cos_ref[...] = jnp.cos(ang)
    sin_ref[...] = jnp.sin(ang)


def _rope_tables(positions):
    n = positions.size
    inv_freq = ROPE_BASE ** (-jnp.arange(HALF, dtype=F32) / HALF)
    rows = 2048
    return pl.pallas_call(
        _rope_kernel,
        grid=(n // rows,),
        in_specs=[pl.BlockSpec((rows, 1), lambda i: (i, 0)),
                  pl.BlockSpec((1, HALF), lambda i: (0, 0))],
        out_specs=[pl.BlockSpec((rows, HALF), lambda i: (i, 0)),
                   pl.BlockSpec((rows, HALF), lambda i: (i, 0))],
        out_shape=[jax.ShapeDtypeStruct((n, HALF), F32)] * 2,
        compiler_params=_params(("parallel",)),
        name="rope_tables",
    )(positions.reshape(n, 1), inv_freq.reshape(1, HALF))


def _norm_kernel(h_ref, g_ref, o_ref):
    o_ref[...] = _rms_rows(h_ref[...], g_ref[...]).astype(o_ref.dtype)


def _norm_rows(h, g, row, dtype):
    n = h.shape[0]
    t = 512
    return pl.pallas_call(
        _norm_kernel,
        grid=(n // t,),
        in_specs=[pl.BlockSpec((t, D_MODEL), lambda i: (i, 0)),
                  pl.BlockSpec((None, 1, D_MODEL), lambda i: (row, 0, 0))],
        out_specs=pl.BlockSpec((t, D_MODEL), lambda i: (i, 0)),
        out_shape=jax.ShapeDtypeStruct((n, D_MODEL), dtype),
        compiler_params=_params(("parallel",)),
        name="norm_rows",
    )(h, g)


def _inproj_kernel(x_ref, w_ref, o_ref, wb_ref):
    @pl.when(pl.program_id(1) == 0)
    def _():
        wb_ref[...] = w_ref[...].astype(BF16)

    for r in range(x_ref.shape[0] // ROW_TILE):
        rs = slice(r * ROW_TILE, (r + 1) * ROW_TILE)
        o_ref[rs, :] = jnp.dot(x_ref[rs, :], wb_ref[...],
                               preferred_element_type=F32).astype(o_ref.dtype)


def _inproj(xn, w_in, layer):
    n = xn.shape[0]
    tn = 1024
    tm = INPROJ_ROWS
    return pl.pallas_call(
        _inproj_kernel,
        grid=(D_IN_TOTAL // tn, n // tm),
        in_specs=[pl.BlockSpec((tm, D_MODEL), lambda j, i: (i, 0)),
                  pl.BlockSpec((None, D_MODEL, tn), lambda j, i: (layer, 0, j))],
        out_specs=pl.BlockSpec((tm, tn), lambda j, i: (i, j)),
        out_shape=jax.ShapeDtypeStruct((n, D_IN_TOTAL), BF16),
        scratch_shapes=[pltpu.VMEM((D_MODEL, tn), BF16)],
        compiler_params=_params(("parallel", "arbitrary")),
        name="inproj",
    )(xn, w_in)


def _retention_kernel(gc_ref, q_ref, k_ref, v_ref, g_ref, cos_ref, sin_ref, dm_ref, xi_ref,
                      zeta_ref, o_ref, qr_ref, qx_ref, kr_ref, kz_ref, st_ref):
    seq = q_ref.shape[0]
    scale = RET_QK_DIM ** -0.5

    def rot(r, c):
        sl = pl.ds(pl.multiple_of(r * RET_BLOCK, RET_BLOCK), RET_BLOCK)
        cos = cos_ref[sl, :]
        sin = sin_ref[sl, :]
        xi = xi_ref[...]
        zeta = zeta_ref[...]
        q = q_ref[sl, :].astype(F32)
        q1, q2 = q[:, :HALF], q[:, HALF:]
        qa = (q1 * cos - q2 * sin) * scale
        qb = (q1 * sin + q2 * cos) * scale
        qr_ref[sl, :HALF] = qa.astype(BF16)
        qr_ref[sl, HALF:] = qb.astype(BF16)
        qx_ref[sl, :HALF] = (qa * xi).astype(BF16)
        qx_ref[sl, HALF:] = (qb * xi).astype(BF16)
        k = k_ref[sl, :].astype(F32)
        k1, k2 = k[:, :HALF], k[:, HALF:]
        ka = k1 * cos - k2 * sin
        kb = k1 * sin + k2 * cos
        kr_ref[sl, :HALF] = ka.astype(BF16)
        kr_ref[sl, HALF:] = kb.astype(BF16)
        kz_ref[sl, :HALF] = (ka * zeta).astype(BF16)
        kz_ref[sl, HALF:] = (kb * zeta).astype(BF16)
        return c
    lax.fori_loop(0, seq // RET_BLOCK, rot, 0)

    st_ref[...] = jnp.zeros_like(st_ref)
    gamma_block = gc_ref[pl.program_id(1)]

    def block(c, carry):
        sl = pl.ds(pl.multiple_of(c * RET_BLOCK, RET_BLOCK), RET_BLOCK)
        vc = v_ref[sl, :]
        scores = lax.dot_general(qr_ref[sl, :], kr_ref[sl, :], (((1,), (1,)), ((), ())),
                                 preferred_element_type=F32) * dm_ref[...]
        inner = jnp.dot(scores.astype(BF16), vc, preferred_element_type=F32)
        state = st_ref[...]
        cross = jnp.dot(qx_ref[sl, :], state.astype(BF16), preferred_element_type=F32)
        st_ref[...] = state * gamma_block + lax.dot_general(
            kz_ref[sl, :], vc, (((0,), (0,)), ((), ())), preferred_element_type=F32)
        o = inner + cross
        mu = jnp.mean(o, axis=-1, keepdims=True)
        d = o - mu
        var = jnp.mean(d * d, axis=-1, keepdims=True)
        on = d * lax.rsqrt(var + EPS)
        g = g_ref[sl, :].astype(F32)
        o_ref[sl, :] = (on * (g * _sigmoid(g))).astype(o_ref.dtype)
        return carry
    lax.fori_loop(0, seq // RET_BLOCK, block, 0, unroll=4)


def _retention_consts():
    h = RET_HEADS
    log_gamma = jnp.log1p(-jnp.power(2.0, -5.0 - jnp.arange(h, dtype=F32)))
    idx = jnp.arange(RET_BLOCK, dtype=F32)
    diff = idx[:, None] - idx[None, :]
    causal = diff >= 0
    decay_mask = jnp.where(causal[None],
                           jnp.exp(jnp.where(causal, diff, 0.0)[None] * log_gamma[:, None, None]),
                           0.0)
    xi = jnp.exp((idx + 1.0)[None] * log_gamma[:, None])
    zeta = jnp.exp((RET_BLOCK - 1.0 - idx)[None] * log_gamma[:, None])
    gamma_block = jnp.exp(RET_BLOCK * log_gamma)
    xi_b = jnp.broadcast_to(xi[:, :, None], (h, RET_BLOCK, HALF))
    zeta_b = jnp.broadcast_to(zeta[:, :, None], (h, RET_BLOCK, HALF))
    return decay_mask, xi_b, zeta_b, gamma_block


def _retention(proj, cos, sin, consts, bsz, seq):
    decay_mask, xi_b, zeta_b, gamma_chunk = consts
    n = proj.shape[0]
    qb, vb = RET_QK_DIM, RET_V_DIM
    return pl.pallas_call(
        _retention_kernel,
        grid=(bsz, RET_HEADS),
        in_specs=[
            pl.BlockSpec(memory_space=pltpu.SMEM),
            pl.BlockSpec((seq, qb), lambda b, h: (b, OFF_Q // qb + h)),
            pl.BlockSpec((seq, qb), lambda b, h: (b, OFF_K // qb + h)),
            pl.BlockSpec((seq, vb), lambda b, h: (b, OFF_V // vb + h)),
            pl.BlockSpec((seq, vb), lambda b, h: (b, OFF_G // vb + h)),
            pl.BlockSpec((seq, HALF), lambda b, h: (b, 0)),
            pl.BlockSpec((seq, HALF), lambda b, h: (b, 0)),
            pl.BlockSpec((None, RET_BLOCK, RET_BLOCK), lambda b, h: (h, 0, 0)),
            pl.BlockSpec((None, RET_BLOCK, HALF), lambda b, h: (h, 0, 0)),
            pl.BlockSpec((None, RET_BLOCK, HALF), lambda b, h: (h, 0, 0)),
        ],
        out_specs=pl.BlockSpec((seq, vb), lambda b, h: (b, h)),
        scratch_shapes=[pltpu.VMEM((seq, qb), BF16), pltpu.VMEM((seq, qb), BF16),
                        pltpu.VMEM((seq, qb), BF16), pltpu.VMEM((seq, qb), BF16),
                        pltpu.VMEM((qb, vb), F32)],
        out_shape=jax.ShapeDtypeStruct((n, RET_HEADS * vb), BF16),
        compiler_params=_params(("parallel", "parallel")),
        name="retention",
    )(gamma_chunk, proj, proj, proj, proj, cos, sin, decay_mask, xi_b, zeta_b)


def _mix_out_kernel(og_ref, cb_ref, cc_ref, ch_ref, ga_ref, gb_ref, h_ref, cw_ref,
                    wr_ref, wc_ref, wo_ref, o_ref, tail_ref, *, tiles_per_seq):
    t = og_ref.shape[0]

    @pl.when(pl.program_id(0) % tiles_per_seq == 0)
    def _():
        tail_ref[...] = jnp.zeros_like(tail_ref)

    yr = jnp.dot(og_ref[...], wr_ref[...], preferred_element_type=F32)
    row = lax.broadcasted_iota(jnp.int32, (t, CONV_LANES), 0)
    yc = None
    for c in range(D_MODEL // CONV_LANES):
        cs = slice(c * CONV_LANES, (c + 1) * CONV_LANES)
        u = cc_ref[:, cs].astype(F32) * ch_ref[:, cs].astype(F32)
        prev1 = tail_ref[7:8, cs]
        prev2 = tail_ref[6:7, cs]
        u1 = jnp.where(row >= 1, pltpu.roll(u, 1, 0), prev1)
        u2 = jnp.where(row >= 2, pltpu.roll(u, 2, 0), jnp.where(row == 1, prev1, prev2))
        w = cw_ref[:, cs]
        y = w[0:1, :] * u2 + w[1:2, :] * u1 + w[2:3, :] * u
        cv = (cb_ref[:, cs].astype(F32) * y).astype(BF16)
        tail_ref[:, cs] = u[t - 8:t, :]
        part = jnp.dot(cv, wc_ref[cs, :], preferred_element_type=F32)
        yc = part if yc is None else yc + part
    ga = ga_ref[...].astype(F32)
    gb = gb_ref[...].astype(F32)
    merged = (_sigmoid(ga) * yr + _sigmoid(gb) * yc).astype(BF16)
    o_ref[...] = h_ref[...] + jnp.dot(merged, wo_ref[...], preferred_element_type=F32)


def _mix_out(og, proj, h, conv_w, w_ret_out, w_conv_out, w_out, layer, seq):
    n = h.shape[0]
    t = MIX_TILE
    rows = lambda i: (i, 0)
    pcol = lambda off: pl.BlockSpec((t, D_MODEL), lambda i: (i, off // D_MODEL))
    wspec = pl.BlockSpec((None, D_MODEL, D_MODEL), lambda i: (layer, 0, 0),
                         pipeline_mode=pl.Buffered(1))
    return pl.pallas_call(
        functools.partial(_mix_out_kernel, tiles_per_seq=seq // t),
        grid=(n // t,),
        in_specs=[pl.BlockSpec((t, D_MODEL), rows),
                  pcol(OFF_CB), pcol(OFF_CC), pcol(OFF_CH), pcol(OFF_GA), pcol(OFF_GB),
                  pl.BlockSpec((t, D_MODEL), rows),
                  pl.BlockSpec((None, CONV_K, D_MODEL), lambda i: (layer, 0, 0)),
                  wspec, wspec, wspec],
        out_specs=pl.BlockSpec((t, D_MODEL), rows),
        out_shape=jax.ShapeDtypeStruct((n, D_MODEL), F32),
        scratch_shapes=[pltpu.VMEM((8, D_MODEL), F32)],
        compiler_params=_params(("arbitrary",)),
        name="mix_out",
    )(og, proj, proj, proj, proj, proj, h, conv_w, w_ret_out, w_conv_out, w_out)


def _normalise_tile(src_ref, g_ref, xb_ref, o_ref, residual):
    def body(r, c):
        sl = pl.ds(pl.multiple_of(r * SUB_ROWS, SUB_ROWS), SUB_ROWS)
        x = src_ref[sl, :]
        xb_ref[sl, :] = _rms_rows(x, g_ref[...]).astype(BF16)
        o_ref[sl, :] = x if residual else jnp.zeros_like(x)
        return c
    lax.fori_loop(0, ROW_TILE // SUB_ROWS, body, 0)


def _swiglu_rows(xb_ref, w1_ref, w3_ref, w2_ref, o_ref, rows):
    x = xb_ref[:rows, :]
    a = jnp.dot(x, w1_ref[...].astype(BF16), preferred_element_type=F32)
    b = jnp.dot(x, w3_ref[...].astype(BF16), preferred_element_type=F32)
    hh = (a * _sigmoid(a) * b).astype(BF16)
    o_ref[:rows, :] += jnp.dot(hh, w2_ref[...].astype(BF16), preferred_element_type=F32)


def _ffn_dense_kernel(x_ref, g_ref, w1_ref, w3_ref, w2_ref, o_ref, xb_ref):
    @pl.when(pl.program_id(1) == 0)
    def _():
        _normalise_tile(x_ref, g_ref, xb_ref, o_ref, True)

    _swiglu_rows(xb_ref, w1_ref, w3_ref, w2_ref, o_ref, ROW_TILE)


def _ffn_dense(x, g, g_row, w1, w3, w2, w_idx):
    rows = x.shape[0]
    tf = FFN_COLS
    nf = w1.shape[-1] // tf
    return pl.pallas_call(
        _ffn_dense_kernel,
        grid=(rows // ROW_TILE, nf),
        in_specs=[
            pl.BlockSpec((ROW_TILE, D_MODEL), lambda i, f: (i, 0), pipeline_mode=pl.Buffered(1)),
            pl.BlockSpec((None, 1, D_MODEL), lambda i, f: (g_row, 0, 0)),
            pl.BlockSpec((None, D_MODEL, tf), lambda i, f: (w_idx, 0, f)),
            pl.BlockSpec((None, D_MODEL, tf), lambda i, f: (w_idx, 0, f)),
            pl.BlockSpec((None, tf, D_MODEL), lambda i, f: (w_idx, f, 0)),
        ],
        out_specs=pl.BlockSpec((ROW_TILE, D_MODEL), lambda i, f: (i, 0)),
        scratch_shapes=[pltpu.VMEM((ROW_TILE, D_MODEL), BF16)],
        out_shape=jax.ShapeDtypeStruct((rows, D_MODEL), F32),
        compiler_params=_params(("parallel", "arbitrary"), FFN_VMEM_LIMIT),
        name="ffn_dense",
    )(x, g, w1, w3, w2)


def _experts_kernel(te_ref, tv_ref, src_ref, nsrc_ref, h_hbm, g_ref, w1_ref, w3_ref, w2_ref,
                    o_ref, land_ref, xb_ref, sem, *, rows_per_step):
    i = pl.program_id(0)
    f = pl.program_id(1)
    land_rows = land_ref.shape[0]

    def row_copy(idx_ref, r):
        tok = idx_ref[jnp.minimum(r, ROW_TILE - 1)]
        return pltpu.make_async_copy(h_hbm.at[pl.ds(tok, 1)], land_ref.at[pl.ds(r, 1)], sem)

    def wait_landing():
        pltpu.make_async_copy(h_hbm.at[pl.ds(0, land_rows)], land_ref, sem).wait()

    @pl.when(jnp.logical_and(i == 0, f == 0))
    def _():
        def body(r, c):
            row_copy(src_ref, r).start()
            return c
        lax.fori_loop(0, land_rows, body, 0)
        wait_landing()

    @pl.when(f == 0)
    def _():
        _normalise_tile(land_ref, g_ref, xb_ref, o_ref, False)

    def step(groups):
        base = f * rows_per_step
        for j in range(rows_per_step):
            row_copy(nsrc_ref, base + j).start()
        if groups:
            _swiglu_rows(xb_ref, w1_ref, w3_ref, w2_ref, o_ref, groups * SUB_ROWS)

    for groups in range(ROW_TILE // SUB_ROWS + 1):
        pl.when(tv_ref[i] == groups)(functools.partial(step, groups))

    @pl.when(f == pl.num_programs(1) - 1)
    def _():
        wait_landing()


def _ffn_experts(h, g, g_row, w1, w3, w2, w_idx, tile_expert, tile_valid, src):
    n_tiles = tile_expert.shape[0]
    tf = FFN_COLS
    nf = w1.shape[-1] // tf
    rows_per_step = -(-ROW_TILE // nf)
    while (rows_per_step * nf) % 8:
        rows_per_step += 1
    land_rows = rows_per_step * nf

    def wcol(i, f, te, tv):
        return jnp.where(tv[i] > 0, f, nf - 1)

    return pl.pallas_call(
        functools.partial(_experts_kernel, rows_per_step=rows_per_step),
        grid_spec=pltpu.PrefetchScalarGridSpec(
            num_scalar_prefetch=2,
            grid=(n_tiles, nf),
            in_specs=[
                pl.BlockSpec((ROW_TILE,), lambda i, f, te, tv: (i,), memory_space=pltpu.SMEM),
                pl.BlockSpec((ROW_TILE,), lambda i, f, te, tv: (jnp.minimum(i + 1, n_tiles - 1),),
                             memory_space=pltpu.SMEM),
                pl.BlockSpec(memory_space=pl.ANY),
                pl.BlockSpec((None, 1, D_MODEL), lambda i, f, te, tv: (g_row, 0, 0)),
                pl.BlockSpec((None, None, D_MODEL, tf),
                             lambda i, f, te, tv: (w_idx, te[i], 0, wcol(i, f, te, tv))),
                pl.BlockSpec((None, None, D_MODEL, tf),
                             lambda i, f, te, tv: (w_idx, te[i], 0, wcol(i, f, te, tv))),
                pl.BlockSpec((None, None, tf, D_MODEL),
                             lambda i, f, te, tv: (w_idx, te[i], wcol(i, f, te, tv), 0)),
            ],
            out_specs=pl.BlockSpec((ROW_TILE, D_MODEL), lambda i, f, te, tv: (i, 0)),
            scratch_shapes=[pltpu.VMEM((land_rows, D_MODEL), F32),
                            pltpu.VMEM((ROW_TILE, D_MODEL), BF16),
                            pltpu.SemaphoreType.DMA(())],
        ),
        out_shape=jax.ShapeDtypeStruct((n_tiles * ROW_TILE, D_MODEL), F32),
        compiler_params=_params(("arbitrary", "arbitrary"), FFN_VMEM_LIMIT),
        name="ffn_experts",
    )(tile_expert, tile_valid, src, src, h, g, w1, w3, w2)


def _route_kernel(h_ref, g_ref, rh_ref, rl_ref, info_ref, info_t_ref, cnt_ref, run_ref):
    t = h_ref.shape[0]

    @pl.when(pl.program_id(0) == 0)
    def _():
        run_ref[...] = jnp.zeros_like(run_ref)

    xn = _rms_rows(h_ref[...], g_ref[...])
    xh = xn.astype(BF16)
    xl = (xn - xh.astype(F32)).astype(BF16)
    logits = (jnp.dot(xh, rh_ref[...], preferred_element_type=F32)
              + jnp.dot(xl, rh_ref[...], preferred_element_type=F32)
              + jnp.dot(xh, rl_ref[...], preferred_element_type=F32))
    lane = lax.broadcasted_iota(jnp.int32, (t, LANES), 1).astype(F32)
    neg = jnp.float32(-jnp.inf)
    logits = jnp.where(lane < N_EXPERTS, logits, neg)
    v1 = jnp.max(logits, axis=-1, keepdims=True)
    e1 = jnp.min(jnp.where(logits == v1, lane, float(LANES)), axis=-1, keepdims=True)
    rest = jnp.where(lane == e1, neg, logits)
    v2 = jnp.max(rest, axis=-1, keepdims=True)
    e2 = jnp.min(jnp.where(rest == v2, lane, float(LANES)), axis=-1, keepdims=True)
    ex = jnp.exp(v2 - v1)
    g1 = 1.0 / (1.0 + ex)
    g2 = ex / (1.0 + ex)

    onehot = jnp.where(jnp.logical_or(lane == e1, lane == e2), 1.0, 0.0)
    r_i = lax.broadcasted_iota(jnp.int32, (t, t), 0)
    c_i = lax.broadcasted_iota(jnp.int32, (t, t), 1)
    tri = jnp.where(c_i <= r_i, 1.0, 0.0).astype(BF16)
    cum = jnp.dot(tri, onehot.astype(BF16), preferred_element_type=F32)
    excl = cum - onehot + run_ref[...]
    rank1 = jnp.sum(jnp.where(lane == e1, excl, 0.0), axis=-1, keepdims=True)
    rank2 = jnp.sum(jnp.where(lane == e2, excl, 0.0), axis=-1, keepdims=True)
    run_ref[...] = run_ref[...] + cum[t - 1:t, :]

    info = jnp.where(lane == 0, e1, 0.0)
    info = jnp.where(lane == 1, e2, info)
    info = jnp.where(lane == 2, g1, info)
    info = jnp.where(lane == 3, g2, info)
    info = jnp.where(lane == 4, rank1, info)
    info = jnp.where(lane == 5, rank2, info)
    info_ref[...] = info
    info_t_ref[...] = info.T[:8, :]
    cnt_ref[...] = jnp.broadcast_to(run_ref[...], cnt_ref.shape)


def _route(h, g, router, layer):
    n = h.shape[0]
    r = jnp.zeros((D_MODEL, LANES), F32).at[:, :N_EXPERTS].set(router)
    rh = r.astype(BF16)
    rl = (r - rh.astype(F32)).astype(BF16)
    t = ROUTE_TILE
    return pl.pallas_call(
        _route_kernel,
        grid=(n // t,),
        in_specs=[pl.BlockSpec((t, D_MODEL), lambda i: (i, 0)),
                  pl.BlockSpec((None, 1, D_MODEL), lambda i: (layer, 0, 0)),
                  pl.BlockSpec((D_MODEL, LANES), lambda i: (0, 0)),
                  pl.BlockSpec((D_MODEL, LANES), lambda i: (0, 0))],
        out_specs=[pl.BlockSpec((t, LANES), lambda i: (i, 0)),
                   pl.BlockSpec((8, t), lambda i: (0, i)),
                   pl.BlockSpec((8, LANES), lambda i: (0, 0))],
        out_shape=[jax.ShapeDtypeStruct((n, LANES), F32),
                   jax.ShapeDtypeStruct((8, n), F32),
                   jax.ShapeDtypeStruct((8, LANES), F32)],
        scratch_shapes=[pltpu.VMEM((1, LANES), F32)],
        compiler_params=_params(("arbitrary",)),
        name="route",
    )(h, g, rh, rl)


def _combine_kernel(pos_ref, npos_ref, h_ref, info_ref, g_ref, ys_ref, o_ref, buf_ref, sem, *,
                    final_norm):
    t = h_ref.shape[0]
    i = pl.program_id(0)
    cur = i % 2

    def gather(idx_ref, half):
        for r in range(t):
            for k in range(TOP_K):
                pltpu.make_async_copy(ys_ref.at[pl.ds(idx_ref[k * t + r], 1)],
                                      buf_ref.at[half, k, pl.ds(r, 1)], sem.at[half]).start()

    @pl.when(i == 0)
    def _():
        gather(pos_ref, 0)

    has_next = i + 1 < pl.num_programs(0)
    for half in range(2):
        pl.when(jnp.logical_and(has_next, cur != half))(functools.partial(gather, npos_ref, half))

    for k in range(TOP_K):
        pltpu.make_async_copy(ys_ref.at[pl.ds(0, t)], buf_ref.at[cur, k], sem.at[cur]).wait()

    info = info_ref[...]
    lane = lax.broadcasted_iota(jnp.int32, info.shape, 1)
    g1 = jnp.sum(jnp.where(lane == 2, info, 0.0), axis=-1, keepdims=True)
    g2 = jnp.sum(jnp.where(lane == 3, info, 0.0), axis=-1, keepdims=True)
    y = h_ref[...] + g1 * buf_ref[cur, 0] + g2 * buf_ref[cur, 1]
    o_ref[...] = _rms_rows(y, g_ref[...]) if final_norm else y


def _combine(h, info, norm_final, ys, pos_tiles, final_norm):
    n = h.shape[0]
    t = COMBINE_TILE
    last = n // t - 1
    return pl.pallas_call(
        functools.partial(_combine_kernel, final_norm=final_norm),
        grid=(n // t,),
        in_specs=[pl.BlockSpec((TOP_K * t,), lambda i: (i,), memory_space=pltpu.SMEM),
                  pl.BlockSpec((TOP_K * t,), lambda i: (jnp.minimum(i + 1, last),),
                               memory_space=pltpu.SMEM),
                  pl.BlockSpec((t, D_MODEL), lambda i: (i, 0)),
                  pl.BlockSpec((t, LANES), lambda i: (i, 0)),
                  pl.BlockSpec((1, D_MODEL), lambda i: (0, 0)),
                  pl.BlockSpec(memory_space=pl.ANY)],
        out_specs=pl.BlockSpec((t, D_MODEL), lambda i: (i, 0)),
        out_shape=jax.ShapeDtypeStruct((n, D_MODEL), F32),
        scratch_shapes=[pltpu.VMEM((2, TOP_K, t, D_MODEL), F32), pltpu.SemaphoreType.DMA((2,))],
        compiler_params=_params(("arbitrary",)),
        name="combine",
    )(pos_tiles, pos_tiles, h, info, norm_final.reshape(1, D_MODEL), ys)


def _pos_tiles(pos1, pos2, tile):
    nt = pos1.shape[0] // tile
    return jnp.stack([pos1.reshape(nt, tile), pos2.reshape(nt, tile)], axis=1).reshape(-1)


def _moe_layer(h, norm_g, layer, router, w1, w3, w2, moe_idx, norm_final, final_norm):
    n = h.shape[0]
    info, info_t, cnt = _route(h, norm_g, router, layer)
    counts = cnt[0, :N_EXPERTS].astype(jnp.int32)
    padded = ((counts + ROW_TILE - 1) // ROW_TILE) * ROW_TILE
    ends = jnp.cumsum(padded)
    starts = ends - padded
    e1 = info_t[0].astype(jnp.int32)
    e2 = info_t[1].astype(jnp.int32)
    pos1 = starts[e1] + info_t[4].astype(jnp.int32)
    pos2 = starts[e2] + info_t[5].astype(jnp.int32)

    n_tiles = (n * TOP_K) // ROW_TILE + N_EXPERTS
    tile_start = jnp.arange(n_tiles, dtype=jnp.int32) * ROW_TILE
    in_use = tile_start < ends[-1]
    tile_expert = jnp.sum((tile_start[:, None] >= ends[None, :]).astype(jnp.int32), axis=1)
    last_expert = jnp.sum((ends[-1] - 1 >= ends).astype(jnp.int32))
    tile_expert = jnp.where(in_use, tile_expert, last_expert).astype(jnp.int32)
    tile_rows = jnp.clip((starts + counts)[tile_expert] - tile_start, 0, ROW_TILE)
    tile_valid = jnp.where(in_use, (tile_rows + SUB_ROWS - 1) // SUB_ROWS, 0).astype(jnp.int32)

    tok = jnp.arange(n, dtype=jnp.int32)
    src = jnp.zeros((n_tiles * ROW_TILE,), jnp.int32).at[jnp.concatenate([pos1, pos2])].set(
        jnp.concatenate([tok, tok]), unique_indices=True)

    ys = _ffn_experts(h, norm_g, layer, w1, w3, w2, moe_idx, tile_expert, tile_valid, src)
    return _combine(h, info, norm_final, ys, _pos_tiles(pos1, pos2, COMBINE_TILE), final_norm)


def kernel(x, positions, norm_mix, norm_ffn, norm_final, w_in, conv_w, w_ret_out, w_conv_out,
           w_out, ffn_w1, ffn_w3, ffn_w2, router, moe_w1, moe_w3, moe_w2):
    bsz, seq, _ = x.shape
    depth = w_in.shape[0]
    n = bsz * seq
    h = x.reshape(n, D_MODEL)
    norm_mix = norm_mix.reshape(depth, 1, D_MODEL)
    norm_ffn = norm_ffn.reshape(depth, 1, D_MODEL)
    cos, sin = _rope_tables(positions)
    consts = _retention_consts()
    w_ret_out, w_conv_out, w_out = (w.astype(BF16) for w in (w_ret_out, w_conv_out, w_out))
    for layer in range(depth):
        proj = _inproj(_norm_rows(h, norm_mix, layer, BF16), w_in, layer)
        og = _retention(proj, cos, sin, consts, bsz, seq)
        h = _mix_out(og, proj, h, conv_w, w_ret_out, w_conv_out, w_out, layer, seq)
        i = layer // 2
        last = layer == depth - 1
        if layer % 2 == 0:
            h = _ffn_dense(h, norm_ffn, layer, ffn_w1, ffn_w3, ffn_w2, i)
            if last:
                h = _norm_rows(h, norm_final.reshape(1, 1, D_MODEL), 0, F32)
        else:
            h = _moe_layer(h, norm_ffn, layer, router[i], moe_w1, moe_w3, moe_w2, i,
                           norm_final, last)
    return h.reshape(bsz, seq, D_MODEL)
```

```python
import functools

import jax
import jax.numpy as jnp
from jax import lax
from jax.experimental import pallas as pl
from jax.experimental.pallas import tpu as pltpu

F32 = jnp.float32
BF16 = jnp.bfloat16

D_MODEL = 2048
RET_HEADS = 4
RET_QK_DIM = 256
RET_V_DIM = 512
RET_BLOCK = 256
ROPE_BASE = 10000.0
CONV_K = 3
N_EXPERTS = 8
TOP_K = 2
EPS = 1e-6
HALF = RET_QK_DIM // 2

OFF_Q, OFF_K, OFF_V, OFF_G = 0, 1024, 2048, 4096
OFF_CB, OFF_CC, OFF_CH, OFF_GA, OFF_GB = 6144, 8192, 10240, 12288, 14336
D_IN_TOTAL = 16384

LANES = 128
VMEM_LIMIT = 56 * 1024 * 1024

ROW_TILE = 1024
INPROJ_ROWS = 2048
SUB_ROWS = 256
MIX_TILE = 256
FFN_COLS = 512
FFN_VMEM_LIMIT = 61 * 1024 * 1024
ROUTE_TILE = 512
CONV_LANES = 512
COMBINE_TILE = 256


def _params(sem, vmem=VMEM_LIMIT):
    return pltpu.CompilerParams(dimension_semantics=sem, vmem_limit_bytes=vmem)


def _rms_rows(x, g):
    ms = jnp.mean(x * x, axis=-1, keepdims=True)
    return x * lax.rsqrt(ms + EPS) * g


def _sigmoid(x):
    return 1.0 / (1.0 + jnp.exp(-x))


def _rope_kernel(pos_ref, invf_ref, cos_ref, sin_ref):
    ang = pos_ref[...].astype(F32) * invf_ref[...]
    cos_ref[...] = jnp.cos(ang)
    sin_ref[...] = jnp.sin(ang)


def _rope_tables(positions):
    n = positions.size
    inv_freq = ROPE_BASE ** (-jnp.arange(HALF, dtype=F32) / HALF)
    rows = 2048
    return pl.pallas_call(
        _rope_kernel,
        grid=(n // rows,),
        in_specs=[pl.BlockSpec((rows, 1), lambda i: (i, 0)),
                  pl.BlockSpec((1, HALF), lambda i: (0, 0))],
        out_specs=[pl.BlockSpec((rows, HALF), lambda i: (i, 0)),
                   pl.BlockSpec((rows, HALF), lambda i: (i, 0))],
        out_shape=[jax.ShapeDtypeStruct((n, HALF), F32)] * 2,
        compiler_params=_params(("parallel",)),
        name="rope_tables",
    )(positions.reshape(n, 1), inv_freq.reshape(1, HALF))


def _norm_kernel(h_ref, g_ref, o_ref):
    o_ref[...] = _rms_rows(h_ref[...], g_ref[...]).astype(o_ref.dtype)


def _norm_rows(h, g, row, dtype):
    n = h.shape[0]
    t = 512
    return pl.pallas_call(
        _norm_kernel,
        grid=(n // t,),
        in_specs=[pl.BlockSpec((t, D_MODEL), lambda i: (i, 0)),
                  pl.BlockSpec((None, 1, D_MODEL), lambda i: (row, 0, 0))],
        out_specs=pl.BlockSpec((t, D_MODEL), lambda i: (i, 0)),
        out_shape=jax.ShapeDtypeStruct((n, D_MODEL), dtype),
        compiler_params=_params(("parallel",)),
        name="norm_rows",
    )(h, g)


def _inproj_kernel(x_ref, w_ref, o_ref, wb_ref):
    @pl.when(pl.program_id(1) == 0)
    def _():
        wb_ref[...] = w_ref[...].astype(BF16)

    for r in range(x_ref.shape[0] // ROW_TILE):
        rs = slice(r * ROW_TILE, (r + 1) * ROW_TILE)
        o_ref[rs, :] = jnp.dot(x_ref[rs, :], wb_ref[...],
                               preferred_element_type=F32).astype(o_ref.dtype)


def _inproj(xn, w_in, layer):
    n = xn.shape[0]
    tn = 1024
    tm = INPROJ_ROWS
    return pl.pallas_call(
        _inproj_kernel,
        grid=(D_IN_TOTAL // tn, n // tm),
        in_specs=[pl.BlockSpec((tm, D_MODEL), lambda j, i: (i, 0)),
                  pl.BlockSpec((None, D_MODEL, tn), lambda j, i: (layer, 0, j))],
        out_specs=pl.BlockSpec((tm, tn), lambda j, i: (i, j)),
        out_shape=jax.ShapeDtypeStruct((n, D_IN_TOTAL), BF16),
        scratch_shapes=[pltpu.VMEM((D_MODEL, tn), BF16)],
        compiler_params=_params(("parallel", "arbitrary")),
        name="inproj",
    )(xn, w_in)


def _retention_kernel(gc_ref, q_ref, k_ref, v_ref, g_ref, cos_ref, sin_ref, dm_ref, xi_ref,
                      zeta_ref, o_ref, qr_ref, qx_ref, kr_ref, kz_ref, st_ref):
    seq = q_ref.shape[0]
    scale = RET_QK_DIM ** -0.5

    def rot(r, c):
        sl = pl.ds(pl.multiple_of(r * RET_BLOCK, RET_BLOCK), RET_BLOCK)
        cos = cos_ref[sl, :]
        sin = sin_ref[sl, :]
        xi = xi_ref[...]
        zeta = zeta_ref[...]
        q = q_ref[sl, :].astype(F32)
        q1, q2 = q[:, :HALF], q[:, HALF:]
        qa = (q1 * cos - q2 * sin) * scale
        qb = (q1 * sin + q2 * cos) * scale
        qr_ref[sl, :HALF] = qa.astype(BF16)
        qr_ref[sl, HALF:] = qb.astype(BF16)
        qx_ref[sl, :HALF] = (qa * xi).astype(BF16)
        qx_ref[sl, HALF:] = (qb * xi).astype(BF16)
        k = k_ref[sl, :].astype(F32)
        k1, k2 = k[:, :HALF], k[:, HALF:]
        ka = k1 * cos - k2 * sin
        kb = k1 * sin + k2 * cos
        kr_ref[sl, :HALF] = ka.astype(BF16)
        kr_ref[sl, HALF:] = kb.astype(BF16)
        kz_ref[sl, :HALF] = (ka * zeta).astype(BF16)
        kz_ref[sl, HALF:] = (kb * zeta).astype(BF16)
        return c
    lax.fori_loop(0, seq // RET_BLOCK, rot, 0)

    st_ref[...] = jnp.zeros_like(st_ref)
    gamma_block = gc_ref[pl.program_id(1)]

    def block(c, carry):
        sl = pl.ds(pl.multiple_of(c * RET_BLOCK, RET_BLOCK), RET_BLOCK)
        vc = v_ref[sl, :]
        scores = lax.dot_general(qr_ref[sl, :], kr_ref[sl, :], (((1,), (1,)), ((), ())),
                                 preferred_element_type=F32) * dm_ref[...]
        inner = jnp.dot(scores.astype(BF16), vc, preferred_element_type=F32)
        state = st_ref[...]
        cross = jnp.dot(qx_ref[sl, :], state.astype(BF16), preferred_element_type=F32)
        st_ref[...] = state * gamma_block + lax.dot_general(
            kz_ref[sl, :], vc, (((0,), (0,)), ((), ())), preferred_element_type=F32)
        o = inner + cross
        mu = jnp.mean(o, axis=-1, keepdims=True)
        d = o - mu
        var = jnp.mean(d * d, axis=-1, keepdims=True)
        on = d * lax.rsqrt(var + EPS)
        g = g_ref[sl, :].astype(F32)
        o_ref[sl, :] = (on * (g * _sigmoid(g))).astype(o_ref.dtype)
        return carry
    lax.fori_loop(0, seq // RET_BLOCK, block, 0, unroll=True)


def _retention_consts():
    h = RET_HEADS
    log_gamma = jnp.log1p(-jnp.power(2.0, -5.0 - jnp.arange(h, dtype=F32)))
    idx = jnp.arange(RET_BLOCK, dtype=F32)
    diff = idx[:, None] - idx[None, :]
    causal = diff >= 0
    decay_mask = jnp.where(causal[None],
                           jnp.exp(jnp.where(causal, diff, 0.0)[None] * log_gamma[:, None, None]),
                           0.0)
    xi = jnp.exp((idx + 1.0)[None] * log_gamma[:, None])
    zeta = jnp.exp((RET_BLOCK - 1.0 - idx)[None] * log_gamma[:, None])
    gamma_block = jnp.exp(RET_BLOCK * log_gamma)
    xi_b = jnp.broadcast_to(xi[:, :, None], (h, RET_BLOCK, HALF))
    zeta_b = jnp.broadcast_to(zeta[:, :, None], (h, RET_BLOCK, HALF))
    return decay_mask, xi_b, zeta_b, gamma_block


def _retention(proj, cos, sin, consts, bsz, seq):
    decay_mask, xi_b, zeta_b, gamma_chunk = consts
    n = proj.shape[0]
    qb, vb = RET_QK_DIM, RET_V_DIM
    return pl.pallas_call(
        _retention_kernel,
        grid=(bsz, RET_HEADS),
        in_specs=[
            pl.BlockSpec(memory_space=pltpu.SMEM),
            pl.BlockSpec((seq, qb), lambda b, h: (b, OFF_Q // qb + h)),
            pl.BlockSpec((seq, qb), lambda b, h: (b, OFF_K // qb + h)),
            pl.BlockSpec((seq, vb), lambda b, h: (b, OFF_V // vb + h)),
            pl.BlockSpec((seq, vb), lambda b, h: (b, OFF_G // vb + h)),
            pl.BlockSpec((seq, HALF), lambda b, h: (b, 0)),
            pl.BlockSpec((seq, HALF), lambda b, h: (b, 0)),
            pl.BlockSpec((None, RET_BLOCK, RET_BLOCK), lambda b, h: (h, 0, 0)),
            pl.BlockSpec((None, RET_BLOCK, HALF), lambda b, h: (h, 0, 0)),
            pl.BlockSpec((None, RET_BLOCK, HALF), lambda b, h: (h, 0, 0)),
        ],
        out_specs=pl.BlockSpec((seq, vb), lambda b, h: (b, h)),
        scratch_shapes=[pltpu.VMEM((seq, qb), BF16), pltpu.VMEM((seq, qb), BF16),
                        pltpu.VMEM((seq, qb), BF16), pltpu.VMEM((seq, qb), BF16),
                        pltpu.VMEM((qb, vb), F32)],
        out_shape=jax.ShapeDtypeStruct((n, RET_HEADS * vb), BF16),
        compiler_params=_params(("parallel", "parallel")),
        name="retention",
    )(gamma_chunk, proj, proj, proj, proj, cos, sin, decay_mask, xi_b, zeta_b)


def _mix_out_kernel(og_ref, cb_ref, cc_ref, ch_ref, ga_ref, gb_ref, h_ref, cw_ref,
                    wr_ref, wc_ref, wo_ref, o_ref, tail_ref, *, tiles_per_seq):
    t = og_ref.shape[0]

    @pl.when(pl.program_id(0) % tiles_per_seq == 0)
    def _():
        tail_ref[...] = jnp.zeros_like(tail_ref)

    yr = jnp.dot(og_ref[...], wr_ref[...], preferred_element_type=F32)
    row = lax.broadcasted_iota(jnp.int32, (t, CONV_LANES), 0)
    yc = None
    for c in range(D_MODEL // CONV_LANES):
        cs = slice(c * CONV_LANES, (c + 1) * CONV_LANES)
        u = cc_ref[:, cs].astype(F32) * ch_ref[:, cs].astype(F32)
        prev1 = tail_ref[7:8, cs]
        prev2 = tail_ref[6:7, cs]
        u1 = jnp.where(row >= 1, pltpu.roll(u, 1, 0), prev1)
        u2 = jnp.where(row >= 2, pltpu.roll(u, 2, 0), jnp.where(row == 1, prev1, prev2))
        w = cw_ref[:, cs]
        y = w[0:1, :] * u2 + w[1:2, :] * u1 + w[2:3, :] * u
        cv = (cb_ref[:, cs].astype(F32) * y).astype(BF16)
        tail_ref[:, cs] = u[t - 8:t, :]
        part = jnp.dot(cv, wc_ref[cs, :], preferred_element_type=F32)
        yc = part if yc is None else yc + part
    ga = ga_ref[...].astype(F32)
    gb = gb_ref[...].astype(F32)
    merged = (_sigmoid(ga) * yr + _sigmoid(gb) * yc).astype(BF16)
    o_ref[...] = h_ref[...] + jnp.dot(merged, wo_ref[...], preferred_element_type=F32)


def _mix_out(og, proj, h, conv_w, w_ret_out, w_conv_out, w_out, layer, seq):
    n = h.shape[0]
    t = MIX_TILE
    rows = lambda i: (i, 0)
    pcol = lambda off: pl.BlockSpec((t, D_MODEL), lambda i: (i, off // D_MODEL))
    wspec = pl.BlockSpec((None, D_MODEL, D_MODEL), lambda i: (layer, 0, 0),
                         pipeline_mode=pl.Buffered(1))
    return pl.pallas_call(
        functools.partial(_mix_out_kernel, tiles_per_seq=seq // t),
        grid=(n // t,),
        in_specs=[pl.BlockSpec((t, D_MODEL), rows),
                  pcol(OFF_CB), pcol(OFF_CC), pcol(OFF_CH), pcol(OFF_GA), pcol(OFF_GB),
                  pl.BlockSpec((t, D_MODEL), rows),
                  pl.BlockSpec((None, CONV_K, D_MODEL), lambda i: (layer, 0, 0)),
                  wspec, wspec, wspec],
        out_specs=pl.BlockSpec((t, D_MODEL), rows),
        out_shape=jax.ShapeDtypeStruct((n, D_MODEL), F32),
        scratch_shapes=[pltpu.VMEM((8, D_MODEL), F32)],
        compiler_params=_params(("arbitrary",)),
        name="mix_out",
    )(og, proj, proj, proj, proj, proj, h, conv_w, w_ret_out, w_conv_out, w_out)


def _normalise_tile(src_ref, g_ref, xb_ref, o_ref, residual):
    def body(r, c):
        sl = pl.ds(pl.multiple_of(r * SUB_ROWS, SUB_ROWS), SUB_ROWS)
        x = src_ref[sl, :]
        xb_ref[sl, :] = _rms_rows(x, g_ref[...]).astype(BF16)
        o_ref[sl, :] = x if residual else jnp.zeros_like(x)
        return c
    lax.fori_loop(0, ROW_TILE // SUB_ROWS, body, 0)


def _swiglu_rows(xb_ref, w1_ref, w3_ref, w2_ref, o_ref, rows):
    x = xb_ref[:rows, :]
    a = jnp.dot(x, w1_ref[...].astype(BF16), preferred_element_type=F32)
    b = jnp.dot(x, w3_ref[...].astype(BF16), preferred_element_type=F32)
    hh = (a * _sigmoid(a) * b).astype(BF16)
    o_ref[:rows, :] += jnp.dot(hh, w2_ref[...].astype(BF16), preferred_element_type=F32)


def _ffn_dense_kernel(x_ref, g_ref, w1_ref, w3_ref, w2_ref, o_ref, xb_ref):
    @pl.when(pl.program_id(1) == 0)
    def _():
        _normalise_tile(x_ref, g_ref, xb_ref, o_ref, True)

    _swiglu_rows(xb_ref, w1_ref, w3_ref, w2_ref, o_ref, ROW_TILE)


def _ffn_dense(x, g, g_row, w1, w3, w2, w_idx):
    rows = x.shape[0]
    tf = FFN_COLS
    nf = w1.shape[-1] // tf
    return pl.pallas_call(
        _ffn_dense_kernel,
        grid=(rows // ROW_TILE, nf),
        in_specs=[
            pl.BlockSpec((ROW_TILE, D_MODEL), lambda i, f: (i, 0), pipeline_mode=pl.Buffered(1)),
            pl.BlockSpec((None, 1, D_MODEL), lambda i, f: (g_row, 0, 0)),
            pl.BlockSpec((None, D_MODEL, tf), lambda i, f: (w_idx, 0, f)),
            pl.BlockSpec((None, D_MODEL, tf), lambda i, f: (w_idx, 0, f)),
            pl.BlockSpec((None, tf, D_MODEL), lambda i, f: (w_idx, f, 0)),
        ],
        out_specs=pl.BlockSpec((ROW_TILE, D_MODEL), lambda i, f: (i, 0)),
        scratch_shapes=[pltpu.VMEM((ROW_TILE, D_MODEL), BF16)],
        out_shape=jax.ShapeDtypeStruct((rows, D_MODEL), F32),
        compiler_params=_params(("parallel", "arbitrary"), FFN_VMEM_LIMIT),
        name="ffn_dense",
    )(x, g, w1, w3, w2)


def _experts_kernel(te_ref, tv_ref, src_ref, nsrc_ref, h_hbm, g_ref, w1_ref, w3_ref, w2_ref,
                    o_ref, land_ref, xb_ref, sem, *, rows_per_step):
    i = pl.program_id(0)
    f = pl.program_id(1)
    land_rows = land_ref.shape[0]

    def row_copy(idx_ref, r):
        tok = idx_ref[jnp.minimum(r, ROW_TILE - 1)]
        return pltpu.make_async_copy(h_hbm.at[pl.ds(tok, 1)], land_ref.at[pl.ds(r, 1)], sem)

    def wait_landing():
        pltpu.make_async_copy(h_hbm.at[pl.ds(0, land_rows)], land_ref, sem).wait()

    @pl.when(jnp.logical_and(i == 0, f == 0))
    def _():
        def body(r, c):
            row_copy(src_ref, r).start()
            return c
        lax.fori_loop(0, land_rows, body, 0)
        wait_landing()

    @pl.when(f == 0)
    def _():
        _normalise_tile(land_ref, g_ref, xb_ref, o_ref, False)

    def step(groups):
        base = f * rows_per_step
        for j in range(rows_per_step):
            row_copy(nsrc_ref, base + j).start()
        _swiglu_rows(xb_ref, w1_ref, w3_ref, w2_ref, o_ref, groups * SUB_ROWS)

    for groups in range(1, ROW_TILE // SUB_ROWS + 1):
        pl.when(tv_ref[i] == groups)(functools.partial(step, groups))

    @pl.when(jnp.logical_and(f == pl.num_programs(1) - 1, tv_ref[i] > 0))
    def _():
        wait_landing()


def _ffn_experts(h, g, g_row, w1, w3, w2, w_idx, tile_expert, tile_valid, src):
    n_tiles = tile_expert.shape[0]
    tf = FFN_COLS
    nf = w1.shape[-1] // tf
    rows_per_step = -(-ROW_TILE // nf)
    while (rows_per_step * nf) % 8:
        rows_per_step += 1
    land_rows = rows_per_step * nf

    def wcol(i, f, te, tv):
        return jnp.where(tv[i] > 0, f, nf - 1)

    return pl.pallas_call(
        functools.partial(_experts_kernel, rows_per_step=rows_per_step),
        grid_spec=pltpu.PrefetchScalarGridSpec(
            num_scalar_prefetch=2,
            grid=(n_tiles, nf),
            in_specs=[
                pl.BlockSpec((ROW_TILE,), lambda i, f, te, tv: (i,), memory_space=pltpu.SMEM),
                pl.BlockSpec((ROW_TILE,), lambda i, f, te, tv: (jnp.minimum(i + 1, n_tiles - 1),),
                             memory_space=pltpu.SMEM),
                pl.BlockSpec(memory_space=pl.ANY),
                pl.BlockSpec((None, 1, D_MODEL), lambda i, f, te, tv: (g_row, 0, 0)),
                pl.BlockSpec((None, None, D_MODEL, tf),
                             lambda i, f, te, tv: (w_idx, te[i], 0, wcol(i, f, te, tv))),
                pl.BlockSpec((None, None, D_MODEL, tf),
                             lambda i, f, te, tv: (w_idx, te[i], 0, wcol(i, f, te, tv))),
                pl.BlockSpec((None, None, tf, D_MODEL),
                             lambda i, f, te, tv: (w_idx, te[i], wcol(i, f, te, tv), 0)),
            ],
            out_specs=pl.BlockSpec((ROW_TILE, D_MODEL), lambda i, f, te, tv: (i, 0)),
            scratch_shapes=[pltpu.VMEM((land_rows, D_MODEL), F32),
                            pltpu.VMEM((ROW_TILE, D_MODEL), BF16),
                            pltpu.SemaphoreType.DMA(())],
        ),
        out_shape=jax.ShapeDtypeStruct((n_tiles * ROW_TILE, D_MODEL), F32),
        compiler_params=_params(("arbitrary", "arbitrary"), FFN_VMEM_LIMIT),
        name="ffn_experts",
    )(tile_expert, tile_valid, src, src, h, g, w1, w3, w2)


def _route_kernel(h_ref, g_ref, rhl_ref, info_ref, info_t_ref, cnt_ref, run_ref):
    t = h_ref.shape[0]

    @pl.when(pl.program_id(0) == 0)
    def _():
        run_ref[...] = jnp.zeros_like(run_ref)

    xn = _rms_rows(h_ref[...], g_ref[...])
    xh = xn.astype(BF16)
    xl = (xn - xh.astype(F32)).astype(BF16)
    hi = jnp.dot(xh, rhl_ref[...], preferred_element_type=F32)
    lo = jnp.dot(xl, rhl_ref[:, :LANES], preferred_element_type=F32)
    logits = hi[:, :LANES] + lo + hi[:, LANES:]
    lane = lax.broadcasted_iota(jnp.int32, (t, LANES), 1).astype(F32)
    neg = jnp.float32(-jnp.inf)
    logits = jnp.where(lane < N_EXPERTS, logits, neg)
    v1 = jnp.max(logits, axis=-1, keepdims=True)
    e1 = jnp.min(jnp.where(logits == v1, lane, float(LANES)), axis=-1, keepdims=True)
    rest = jnp.where(lane == e1, neg, logits)
    v2 = jnp.max(rest, axis=-1, keepdims=True)
    e2 = jnp.min(jnp.where(rest == v2, lane, float(LANES)), axis=-1, keepdims=True)
    ex = jnp.exp(v2 - v1)
    g1 = 1.0 / (1.0 + ex)
    g2 = ex / (1.0 + ex)

    onehot = jnp.where(jnp.logical_or(lane == e1, lane == e2), 1.0, 0.0)
    r_i = lax.broadcasted_iota(jnp.int32, (t, t), 0)
    c_i = lax.broadcasted_iota(jnp.int32, (t, t), 1)
    tri = jnp.where(c_i <= r_i, 1.0, 0.0).astype(BF16)
    cum = jnp.dot(tri, onehot.astype(BF16), preferred_element_type=F32)
    excl = cum - onehot + run_ref[...]
    rank1 = jnp.sum(jnp.where(lane == e1, excl, 0.0), axis=-1, keepdims=True)
    rank2 = jnp.sum(jnp.where(lane == e2, excl, 0.0), axis=-1, keepdims=True)
    run_ref[...] = run_ref[...] + cum[t - 1:t, :]

    info = jnp.where(lane == 0, e1, 0.0)
    info = jnp.where(lane == 1, e2, info)
    info = jnp.where(lane == 2, g1, info)
    info = jnp.where(lane == 3, g2, info)
    info = jnp.where(lane == 4, rank1, info)
    info = jnp.where(lane == 5, rank2, info)
    info_ref[...] = info
    info_t_ref[...] = info.T[:8, :]
    cnt_ref[...] = jnp.broadcast_to(run_ref[...], cnt_ref.shape)


def _route(h, g, router, layer):
    n = h.shape[0]
    r = jnp.zeros((D_MODEL, LANES), F32).at[:, :N_EXPERTS].set(router)
    rh = r.astype(BF16)
    rl = (r - rh.astype(F32)).astype(BF16)
    rhl = jnp.concatenate([rh, rl], axis=1)
    t = ROUTE_TILE
    return pl.pallas_call(
        _route_kernel,
        grid=(n // t,),
        in_specs=[pl.BlockSpec((t, D_MODEL), lambda i: (i, 0)),
                  pl.BlockSpec((None, 1, D_MODEL), lambda i: (layer, 0, 0)),
                  pl.BlockSpec((D_MODEL, 2 * LANES), lambda i: (0, 0))],
        out_specs=[pl.BlockSpec((t, LANES), lambda i: (i, 0)),
                   pl.BlockSpec((8, t), lambda i: (0, i)),
                   pl.BlockSpec((8, LANES), lambda i: (0, 0))],
        out_shape=[jax.ShapeDtypeStruct((n, LANES), F32),
                   jax.ShapeDtypeStruct((8, n), F32),
                   jax.ShapeDtypeStruct((8, LANES), F32)],
        scratch_shapes=[pltpu.VMEM((1, LANES), F32)],
        compiler_params=_params(("arbitrary",)),
        name="route",
    )(h, g, rhl)


def _combine_kernel(pos_ref, npos_ref, h_ref, info_ref, g_ref, ys_ref, o_ref, buf_ref, sem, *,
                    final_norm):
    t = h_ref.shape[0]
    i = pl.program_id(0)
    cur = i % 2

    def gather(idx_ref, half):
        for r in range(t):
            for k in range(TOP_K):
                pltpu.make_async_copy(ys_ref.at[pl.ds(idx_ref[k * t + r], 1)],
                                      buf_ref.at[half, k, pl.ds(r, 1)], sem.at[half]).start()

    @pl.when(i == 0)
    def _():
        gather(pos_ref, 0)

    has_next = i + 1 < pl.num_programs(0)
    for half in range(2):
        pl.when(jnp.logical_and(has_next, cur != half))(functools.partial(gather, npos_ref, half))

    for k in range(TOP_K):
        pltpu.make_async_copy(ys_ref.at[pl.ds(0, t)], buf_ref.at[cur, k], sem.at[cur]).wait()

    info = info_ref[...]
    lane = lax.broadcasted_iota(jnp.int32, info.shape, 1)
    g1 = jnp.sum(jnp.where(lane == 2, info, 0.0), axis=-1, keepdims=True)
    g2 = jnp.sum(jnp.where(lane == 3, info, 0.0), axis=-1, keepdims=True)
    y = h_ref[...] + g1 * buf_ref[cur, 0] + g2 * buf_ref[cur, 1]
    o_ref[...] = _rms_rows(y, g_ref[...]) if final_norm else y


def _combine(h, info, norm_final, ys, pos_tiles, final_norm):
    n = h.shape[0]
    t = COMBINE_TILE
    last = n // t - 1
    return pl.pallas_call(
        functools.partial(_combine_kernel, final_norm=final_norm),
        grid=(n // t,),
        in_specs=[pl.BlockSpec((TOP_K * t,), lambda i: (i,), memory_space=pltpu.SMEM),
                  pl.BlockSpec((TOP_K * t,), lambda i: (jnp.minimum(i + 1, last),),
                               memory_space=pltpu.SMEM),
                  pl.BlockSpec((t, D_MODEL), lambda i: (i, 0)),
                  pl.BlockSpec((t, LANES), lambda i: (i, 0)),
                  pl.BlockSpec((1, D_MODEL), lambda i: (0, 0)),
                  pl.BlockSpec(memory_space=pl.ANY)],
        out_specs=pl.BlockSpec((t, D_MODEL), lambda i: (i, 0)),
        out_shape=jax.ShapeDtypeStruct((n, D_MODEL), F32),
        scratch_shapes=[pltpu.VMEM((2, TOP_K, t, D_MODEL), F32), pltpu.SemaphoreType.DMA((2,))],
        compiler_params=_params(("arbitrary",)),
        name="combine",
    )(pos_tiles, pos_tiles, h, info, norm_final.reshape(1, D_MODEL), ys)


def _pos_tiles(pos1, pos2, tile):
    nt = pos1.shape[0] // tile
    return jnp.stack([pos1.reshape(nt, tile), pos2.reshape(nt, tile)], axis=1).reshape(-1)


def _moe_layer(h, norm_g, layer, router, w1, w3, w2, moe_idx, norm_final, final_norm):
    n = h.shape[0]
    info, info_t, cnt = _route(h, norm_g, router, layer)
    counts = cnt[0, :N_EXPERTS].astype(jnp.int32)
    padded = ((counts + ROW_TILE - 1) // ROW_TILE) * ROW_TILE
    ends = jnp.cumsum(padded)
    starts = ends - padded
    e1 = info_t[0].astype(jnp.int32)
    e2 = info_t[1].astype(jnp.int32)
    pos1 = starts[e1] + info_t[4].astype(jnp.int32)
    pos2 = starts[e2] + info_t[5].astype(jnp.int32)

    n_tiles = (n * TOP_K) // ROW_TILE + N_EXPERTS
    tile_start = jnp.arange(n_tiles, dtype=jnp.int32) * ROW_TILE
    in_use = tile_start < ends[-1]
    tile_expert = jnp.sum((tile_start[:, None] >= ends[None, :]).astype(jnp.int32), axis=1)
    last_expert = jnp.sum((ends[-1] - 1 >= ends).astype(jnp.int32))
    tile_expert = jnp.where(in_use, tile_expert, last_expert).astype(jnp.int32)
    tile_rows = jnp.clip((starts + counts)[tile_expert] - tile_start, 0, ROW_TILE)
    tile_valid = jnp.where(in_use, (tile_rows + SUB_ROWS - 1) // SUB_ROWS, 0).astype(jnp.int32)

    tok = jnp.arange(n, dtype=jnp.int32)
    src = jnp.zeros((n_tiles * ROW_TILE,), jnp.int32).at[jnp.concatenate([pos1, pos2])].set(
        jnp.concatenate([tok, tok]), unique_indices=True)

    ys = _ffn_experts(h, norm_g, layer, w1, w3, w2, moe_idx, tile_expert, tile_valid, src)
    return _combine(h, info, norm_final, ys, _pos_tiles(pos1, pos2, COMBINE_TILE), final_norm)


def kernel(x, positions, norm_mix, norm_ffn, norm_final, w_in, conv_w, w_ret_out, w_conv_out,
           w_out, ffn_w1, ffn_w3, ffn_w2, router, moe_w1, moe_w3, moe_w2):
    bsz, seq, _ = x.shape
    depth = w_in.shape[0]
    n = bsz * seq
    h = x.reshape(n, D_MODEL)
    norm_mix = norm_mix.reshape(depth, 1, D_MODEL)
    norm_ffn = norm_ffn.reshape(depth, 1, D_MODEL)
    cos, sin = _rope_tables(positions)
    consts = _retention_consts()
    w_ret_out, w_conv_out, w_out = (w.astype(BF16) for w in (w_ret_out, w_conv_out, w_out))
    for layer in range(depth):
        proj = _inproj(_norm_rows(h, norm_mix, layer, BF16), w_in, layer)
        og = _retention(proj, cos, sin, consts, bsz, seq)
        h = _mix_out(og, proj, h, conv_w, w_ret_out, w_conv_out, w_out, layer, seq)
        i = layer // 2
        last = layer == depth - 1
        if layer % 2 == 0:
            h = _ffn_dense(h, norm_ffn, layer, ffn_w1, ffn_w3, ffn_w2, i)
            if last:
                h = _norm_rows(h, norm_final.reshape(1, 1, D_MODEL), 0, F32)
        else:
            h = _moe_layer(h, norm_ffn, layer, router[i], moe_w1, moe_w3, moe_w2, i,
                           norm_final, last)
    return h.reshape(bsz, seq, D_MODEL)
```

```python
import functools

import jax
import jax.numpy as jnp
from jax import lax
from jax.experimental import pallas as pl
from jax.experimental.pallas import tpu as pltpu

F32 = jnp.float32
BF16 = jnp.bfloat16

D_MODEL = 2048
RET_HEADS = 4
RET_QK_DIM = 256
RET_V_DIM = 512
RET_BLOCK = 256
ROPE_BASE = 10000.0
CONV_K = 3
N_EXPERTS = 8
TOP_K = 2
EPS = 1e-6
HALF = RET_QK_DIM // 2

OFF_Q, OFF_K, OFF_V, OFF_G = 0, 1024, 2048, 4096
OFF_CB, OFF_CC, OFF_CH, OFF_GA, OFF_GB = 6144, 8192, 10240, 12288, 14336
D_IN_TOTAL = 16384

LANES = 128
VMEM_LIMIT = 56 * 1024 * 1024

ROW_TILE = 1024
INPROJ_ROWS = 2048
SUB_ROWS = 256
MIX_TILE = 256
FFN_COLS = 512
FFN_VMEM_LIMIT = 61 * 1024 * 1024
ROUTE_TILE = 512
CONV_LANES = 512
COMBINE_TILE = 256
INVERT_TILE = 2048


def _params(sem, vmem=VMEM_LIMIT):
    return pltpu.CompilerParams(dimension_semantics=sem, vmem_limit_bytes=vmem)


def _rms_rows(x, g):
    ms = jnp.mean(x * x, axis=-1, keepdims=True)
    return x * lax.rsqrt(ms + EPS) * g


def _sigmoid(x):
    return 1.0 / (1.0 + jnp.exp(-x))


def _rope_kernel(pos_ref, invf_ref, cos_ref, sin_ref):
    ang = pos_ref[...].astype(F32) * invf_ref[...]
    cos_ref[...] = jnp.cos(ang)
    sin_ref[...] = jnp.sin(ang)


def _rope_tables(positions):
    n = positions.size
    inv_freq = ROPE_BASE ** (-jnp.arange(HALF, dtype=F32) / HALF)
    rows = 2048
    return pl.pallas_call(
        _rope_kernel,
        grid=(n // rows,),
        in_specs=[pl.BlockSpec((rows, 1), lambda i: (i, 0)),
                  pl.BlockSpec((1, HALF), lambda i: (0, 0))],
        out_specs=[pl.BlockSpec((rows, HALF), lambda i: (i, 0)),
                   pl.BlockSpec((rows, HALF), lambda i: (i, 0))],
        out_shape=[jax.ShapeDtypeStruct((n, HALF), F32)] * 2,
        compiler_params=_params(("parallel",)),
        name="rope_tables",
    )(positions.reshape(n, 1), inv_freq.reshape(1, HALF))


def _norm_kernel(h_ref, g_ref, o_ref):
    o_ref[...] = _rms_rows(h_ref[...], g_ref[...]).astype(o_ref.dtype)


def _norm_rows(h, g, row, dtype):
    n = h.shape[0]
    t = 512
    return pl.pallas_call(
        _norm_kernel,
        grid=(n // t,),
        in_specs=[pl.BlockSpec((t, D_MODEL), lambda i: (i, 0)),
                  pl.BlockSpec((None, 1, D_MODEL), lambda i: (row, 0, 0))],
        out_specs=pl.BlockSpec((t, D_MODEL), lambda i: (i, 0)),
        out_shape=jax.ShapeDtypeStruct((n, D_MODEL), dtype),
        compiler_params=_params(("parallel",)),
        name="norm_rows",
    )(h, g)


def _inproj_kernel(x_ref, w_ref, o_ref, wb_ref):
    @pl.when(pl.program_id(1) == 0)
    def _():
        wb_ref[...] = w_ref[...].astype(BF16)

    for r in range(x_ref.shape[0] // ROW_TILE):
        rs = slice(r * ROW_TILE, (r + 1) * ROW_TILE)
        o_ref[rs, :] = jnp.dot(x_ref[rs, :], wb_ref[...],
                               preferred_element_type=F32).astype(o_ref.dtype)


def _inproj(xn, w_in, layer):
    n = xn.shape[0]
    tn = 1024
    tm = INPROJ_ROWS
    return pl.pallas_call(
        _inproj_kernel,
        grid=(D_IN_TOTAL // tn, n // tm),
        in_specs=[pl.BlockSpec((tm, D_MODEL), lambda j, i: (i, 0)),
                  pl.BlockSpec((None, D_MODEL, tn), lambda j, i: (layer, 0, j))],
        out_specs=pl.BlockSpec((tm, tn), lambda j, i: (i, j)),
        out_shape=jax.ShapeDtypeStruct((n, D_IN_TOTAL), BF16),
        scratch_shapes=[pltpu.VMEM((D_MODEL, tn), BF16)],
        compiler_params=_params(("parallel", "arbitrary")),
        name="inproj",
    )(xn, w_in)


def _retention_kernel(gc_ref, q_ref, k_ref, v_ref, g_ref, cos_ref, sin_ref, dm_ref, xi_ref,
                      zeta_ref, o_ref, qr_ref, qx_ref, kr_ref, kz_ref, st_ref):
    seq = q_ref.shape[0]
    scale = RET_QK_DIM ** -0.5

    def rot(r, c):
        sl = pl.ds(pl.multiple_of(r * RET_BLOCK, RET_BLOCK), RET_BLOCK)
        cos = cos_ref[sl, :]
        sin = sin_ref[sl, :]
        xi = xi_ref[...]
        zeta = zeta_ref[...]
        q = q_ref[sl, :].astype(F32)
        q1, q2 = q[:, :HALF], q[:, HALF:]
        qa = (q1 * cos - q2 * sin) * scale
        qb = (q1 * sin + q2 * cos) * scale
        qr_ref[sl, :HALF] = qa.astype(BF16)
        qr_ref[sl, HALF:] = qb.astype(BF16)
        qx_ref[sl, :HALF] = (qa * xi).astype(BF16)
        qx_ref[sl, HALF:] = (qb * xi).astype(BF16)
        k = k_ref[sl, :].astype(F32)
        k1, k2 = k[:, :HALF], k[:, HALF:]
        ka = k1 * cos - k2 * sin
        kb = k1 * sin + k2 * cos
        kr_ref[sl, :HALF] = ka.astype(BF16)
        kr_ref[sl, HALF:] = kb.astype(BF16)
        kz_ref[sl, :HALF] = (ka * zeta).astype(BF16)
        kz_ref[sl, HALF:] = (kb * zeta).astype(BF16)
        return c
    lax.fori_loop(0, seq // RET_BLOCK, rot, 0)

    st_ref[...] = jnp.zeros_like(st_ref)
    gamma_block = gc_ref[pl.program_id(1)]

    def block(c, carry):
        sl = pl.ds(pl.multiple_of(c * RET_BLOCK, RET_BLOCK), RET_BLOCK)
        vc = v_ref[sl, :]
        scores = lax.dot_general(qr_ref[sl, :], kr_ref[sl, :], (((1,), (1,)), ((), ())),
                                 preferred_element_type=F32) * dm_ref[...]
        inner = jnp.dot(scores.astype(BF16), vc, preferred_element_type=F32)
        state = st_ref[...]
        cross = jnp.dot(qx_ref[sl, :], state.astype(BF16), preferred_element_type=F32)
        st_ref[...] = state * gamma_block + lax.dot_general(
            kz_ref[sl, :], vc, (((0,), (0,)), ((), ())), preferred_element_type=F32)
        o = inner + cross
        mu = jnp.mean(o, axis=-1, keepdims=True)
        d = o - mu
        var = jnp.mean(d * d, axis=-1, keepdims=True)
        on = d * lax.rsqrt(var + EPS)
        g = g_ref[sl, :].astype(F32)
        o_ref[sl, :] = (on * (g * _sigmoid(g))).astype(o_ref.dtype)
        return carry
    lax.fori_loop(0, seq // RET_BLOCK, block, 0, unroll=True)


def _retention_consts():
    h = RET_HEADS
    log_gamma = jnp.log1p(-jnp.power(2.0, -5.0 - jnp.arange(h, dtype=F32)))
    idx = jnp.arange(RET_BLOCK, dtype=F32)
    diff = idx[:, None] - idx[None, :]
    causal = diff >= 0
    decay_mask = jnp.where(causal[None],
                           jnp.exp(jnp.where(causal, diff, 0.0)[None] * log_gamma[:, None, None]),
                           0.0)
    xi = jnp.exp((idx + 1.0)[None] * log_gamma[:, None])
    zeta = jnp.exp((RET_BLOCK - 1.0 - idx)[None] * log_gamma[:, None])
    gamma_block = jnp.exp(RET_BLOCK * log_gamma)
    xi_b = jnp.broadcast_to(xi[:, :, None], (h, RET_BLOCK, HALF))
    zeta_b = jnp.broadcast_to(zeta[:, :, None], (h, RET_BLOCK, HALF))
    return decay_mask, xi_b, zeta_b, gamma_block


def _retention(proj, cos, sin, consts, bsz, seq):
    decay_mask, xi_b, zeta_b, gamma_chunk = consts
    n = proj.shape[0]
    qb, vb = RET_QK_DIM, RET_V_DIM
    return pl.pallas_call(
        _retention_kernel,
        grid=(bsz, RET_HEADS),
        in_specs=[
            pl.BlockSpec(memory_space=pltpu.SMEM),
            pl.BlockSpec((seq, qb), lambda b, h: (b, OFF_Q // qb + h)),
            pl.BlockSpec((seq, qb), lambda b, h: (b, OFF_K // qb + h)),
            pl.BlockSpec((seq, vb), lambda b, h: (b, OFF_V // vb + h)),
            pl.BlockSpec((seq, vb), lambda b, h: (b, OFF_G // vb + h)),
            pl.BlockSpec((seq, HALF), lambda b, h: (b, 0)),
            pl.BlockSpec((seq, HALF), lambda b, h: (b, 0)),
            pl.BlockSpec((None, RET_BLOCK, RET_BLOCK), lambda b, h: (h, 0, 0)),
            pl.BlockSpec((None, RET_BLOCK, HALF), lambda b, h: (h, 0, 0)),
            pl.BlockSpec((None, RET_BLOCK, HALF), lambda b, h: (h, 0, 0)),
        ],
        out_specs=pl.BlockSpec((seq, vb), lambda b, h: (b, h)),
        scratch_shapes=[pltpu.VMEM((seq, qb), BF16), pltpu.VMEM((seq, qb), BF16),
                        pltpu.VMEM((seq, qb), BF16), pltpu.VMEM((seq, qb), BF16),
                        pltpu.VMEM((qb, vb), F32)],
        out_shape=jax.ShapeDtypeStruct((n, RET_HEADS * vb), BF16),
        compiler_params=_params(("parallel", "parallel")),
        name="retention",
    )(gamma_chunk, proj, proj, proj, proj, cos, sin, decay_mask, xi_b, zeta_b)


def _mix_out_kernel(og_ref, cb_ref, cc_ref, ch_ref, ga_ref, gb_ref, h_ref, cw_ref,
                    wr_ref, wc_ref, wo_ref, o_ref, tail_ref, *, tiles_per_seq):
    t = og_ref.shape[0]

    @pl.when(pl.program_id(0) % tiles_per_seq == 0)
    def _():
        tail_ref[...] = jnp.zeros_like(tail_ref)

    yr = jnp.dot(og_ref[...], wr_ref[...], preferred_element_type=F32)
    row = lax.broadcasted_iota(jnp.int32, (t, CONV_LANES), 0)
    yc = None
    for c in range(D_MODEL // CONV_LANES):
        cs = slice(c * CONV_LANES, (c + 1) * CONV_LANES)
        u = cc_ref[:, cs].astype(F32) * ch_ref[:, cs].astype(F32)
        prev1 = tail_ref[7:8, cs]
        prev2 = tail_ref[6:7, cs]
        u1 = jnp.where(row >= 1, pltpu.roll(u, 1, 0), prev1)
        u2 = jnp.where(row >= 2, pltpu.roll(u, 2, 0), jnp.where(row == 1, prev1, prev2))
        w = cw_ref[:, cs]
        y = w[0:1, :] * u2 + w[1:2, :] * u1 + w[2:3, :] * u
        cv = (cb_ref[:, cs].astype(F32) * y).astype(BF16)
        tail_ref[:, cs] = u[t - 8:t, :]
        part = jnp.dot(cv, wc_ref[cs, :], preferred_element_type=F32)
        yc = part if yc is None else yc + part
    ga = ga_ref[...].astype(F32)
    gb = gb_ref[...].astype(F32)
    merged = (_sigmoid(ga) * yr + _sigmoid(gb) * yc).astype(BF16)
    o_ref[...] = h_ref[...] + jnp.dot(merged, wo_ref[...], preferred_element_type=F32)


def _mix_out(og, proj, h, conv_w, w_ret_out, w_conv_out, w_out, layer, seq):
    n = h.shape[0]
    t = MIX_TILE
    rows = lambda i: (i, 0)
    pcol = lambda off: pl.BlockSpec((t, D_MODEL), lambda i: (i, off // D_MODEL))
    wspec = pl.BlockSpec((None, D_MODEL, D_MODEL), lambda i: (layer, 0, 0),
                         pipeline_mode=pl.Buffered(1))
    return pl.pallas_call(
        functools.partial(_mix_out_kernel, tiles_per_seq=seq // t),
        grid=(n // t,),
        in_specs=[pl.BlockSpec((t, D_MODEL), rows),
                  pcol(OFF_CB), pcol(OFF_CC), pcol(OFF_CH), pcol(OFF_GA), pcol(OFF_GB),
                  pl.BlockSpec((t, D_MODEL), rows),
                  pl.BlockSpec((None, CONV_K, D_MODEL), lambda i: (layer, 0, 0)),
                  wspec, wspec, wspec],
        out_specs=pl.BlockSpec((t, D_MODEL), rows),
        out_shape=jax.ShapeDtypeStruct((n, D_MODEL), F32),
        scratch_shapes=[pltpu.VMEM((8, D_MODEL), F32)],
        compiler_params=_params(("arbitrary",)),
        name="mix_out",
    )(og, proj, proj, proj, proj, proj, h, conv_w, w_ret_out, w_conv_out, w_out)


def _normalise_tile(src_ref, g_ref, xb_ref, o_ref, residual):
    def body(r, c):
        sl = pl.ds(pl.multiple_of(r * SUB_ROWS, SUB_ROWS), SUB_ROWS)
        x = src_ref[sl, :]
        xb_ref[sl, :] = _rms_rows(x, g_ref[...]).astype(BF16)
        o_ref[sl, :] = x if residual else jnp.zeros_like(x)
        return c
    lax.fori_loop(0, ROW_TILE // SUB_ROWS, body, 0)


def _swiglu_rows(xb_ref, w1_ref, w3_ref, w2_ref, o_ref, rows):
    x = xb_ref[:rows, :]
    a = jnp.dot(x, w1_ref[...].astype(BF16), preferred_element_type=F32)
    b = jnp.dot(x, w3_ref[...].astype(BF16), preferred_element_type=F32)
    hh = (a * _sigmoid(a) * b).astype(BF16)
    o_ref[:rows, :] += jnp.dot(hh, w2_ref[...].astype(BF16), preferred_element_type=F32)


def _ffn_dense_kernel(x_ref, g_ref, w1_ref, w3_ref, w2_ref, o_ref, xb_ref):
    @pl.when(pl.program_id(1) == 0)
    def _():
        _normalise_tile(x_ref, g_ref, xb_ref, o_ref, True)

    _swiglu_rows(xb_ref, w1_ref, w3_ref, w2_ref, o_ref, ROW_TILE)


def _ffn_dense(x, g, g_row, w1, w3, w2, w_idx):
    rows = x.shape[0]
    tf = FFN_COLS
    nf = w1.shape[-1] // tf
    return pl.pallas_call(
        _ffn_dense_kernel,
        grid=(rows // ROW_TILE, nf),
        in_specs=[
            pl.BlockSpec((ROW_TILE, D_MODEL), lambda i, f: (i, 0), pipeline_mode=pl.Buffered(1)),
            pl.BlockSpec((None, 1, D_MODEL), lambda i, f: (g_row, 0, 0)),
            pl.BlockSpec((None, D_MODEL, tf), lambda i, f: (w_idx, 0, f)),
            pl.BlockSpec((None, D_MODEL, tf), lambda i, f: (w_idx, 0, f)),
            pl.BlockSpec((None, tf, D_MODEL), lambda i, f: (w_idx, f, 0)),
        ],
        out_specs=pl.BlockSpec((ROW_TILE, D_MODEL), lambda i, f: (i, 0)),
        scratch_shapes=[pltpu.VMEM((ROW_TILE, D_MODEL), BF16)],
        out_shape=jax.ShapeDtypeStruct((rows, D_MODEL), F32),
        compiler_params=_params(("parallel", "arbitrary"), FFN_VMEM_LIMIT),
        name="ffn_dense",
    )(x, g, w1, w3, w2)


def _experts_kernel(te_ref, tv_ref, src_ref, nsrc_ref, h_hbm, g_ref, w1_ref, w3_ref, w2_ref,
                    o_ref, land_ref, xb_ref, sem, *, rows_per_step):
    i = pl.program_id(0)
    f = pl.program_id(1)
    land_rows = land_ref.shape[0]

    def row_copy(idx_ref, r):
        tok = idx_ref[jnp.minimum(r, ROW_TILE - 1)]
        return pltpu.make_async_copy(h_hbm.at[pl.ds(tok, 1)], land_ref.at[pl.ds(r, 1)], sem)

    def wait_landing():
        pltpu.make_async_copy(h_hbm.at[pl.ds(0, land_rows)], land_ref, sem).wait()

    @pl.when(jnp.logical_and(i == 0, f == 0))
    def _():
        def body(r, c):
            row_copy(src_ref, r).start()
            return c
        lax.fori_loop(0, land_rows, body, 0)
        wait_landing()

    @pl.when(f == 0)
    def _():
        _normalise_tile(land_ref, g_ref, xb_ref, o_ref, False)

    def step(groups):
        base = f * rows_per_step
        for j in range(rows_per_step):
            row_copy(nsrc_ref, base + j).start()
        _swiglu_rows(xb_ref, w1_ref, w3_ref, w2_ref, o_ref, groups * SUB_ROWS)

    for groups in range(1, ROW_TILE // SUB_ROWS + 1):
        pl.when(tv_ref[i] == groups)(functools.partial(step, groups))

    @pl.when(jnp.logical_and(f == pl.num_programs(1) - 1, tv_ref[i] > 0))
    def _():
        wait_landing()


def _ffn_experts(h, g, g_row, w1, w3, w2, w_idx, tile_expert, tile_valid, src):
    n_tiles = tile_expert.shape[0]
    tf = FFN_COLS
    nf = w1.shape[-1] // tf
    rows_per_step = -(-ROW_TILE // nf)
    while (rows_per_step * nf) % 8:
        rows_per_step += 1
    land_rows = rows_per_step * nf

    def wcol(i, f, te, tv):
        return jnp.where(tv[i] > 0, f, nf - 1)

    return pl.pallas_call(
        functools.partial(_experts_kernel, rows_per_step=rows_per_step),
        grid_spec=pltpu.PrefetchScalarGridSpec(
            num_scalar_prefetch=2,
            grid=(n_tiles, nf),
            in_specs=[
                pl.BlockSpec((ROW_TILE,), lambda i, f, te, tv: (i,), memory_space=pltpu.SMEM),
                pl.BlockSpec((ROW_TILE,), lambda i, f, te, tv: (jnp.minimum(i + 1, n_tiles - 1),),
                             memory_space=pltpu.SMEM),
                pl.BlockSpec(memory_space=pl.ANY),
                pl.BlockSpec((None, 1, D_MODEL), lambda i, f, te, tv: (g_row, 0, 0)),
                pl.BlockSpec((None, None, D_MODEL, tf),
                             lambda i, f, te, tv: (w_idx, te[i], 0, wcol(i, f, te, tv))),
                pl.BlockSpec((None, None, D_MODEL, tf),
                             lambda i, f, te, tv: (w_idx, te[i], 0, wcol(i, f, te, tv))),
                pl.BlockSpec((None, None, tf, D_MODEL),
                             lambda i, f, te, tv: (w_idx, te[i], wcol(i, f, te, tv), 0)),
            ],
            out_specs=pl.BlockSpec((ROW_TILE, D_MODEL), lambda i, f, te, tv: (i, 0)),
            scratch_shapes=[pltpu.VMEM((land_rows, D_MODEL), F32),
                            pltpu.VMEM((ROW_TILE, D_MODEL), BF16),
                            pltpu.SemaphoreType.DMA(())],
        ),
        out_shape=jax.ShapeDtypeStruct((n_tiles * ROW_TILE, D_MODEL), F32),
        compiler_params=_params(("arbitrary", "arbitrary"), FFN_VMEM_LIMIT),
        name="ffn_experts",
    )(tile_expert, tile_valid, src, src, h, g, w1, w3, w2)


def _route_kernel(h_ref, g_ref, rhl_ref, info_ref, info_t_ref, cnt_ref, run_ref):
    t = h_ref.shape[0]

    @pl.when(pl.program_id(0) == 0)
    def _():
        run_ref[...] = jnp.zeros_like(run_ref)

    xn = _rms_rows(h_ref[...], g_ref[...])
    xh = xn.astype(BF16)
    xl = (xn - xh.astype(F32)).astype(BF16)
    hi = jnp.dot(xh, rhl_ref[...], preferred_element_type=F32)
    lo = jnp.dot(xl, rhl_ref[:, :LANES], preferred_element_type=F32)
    logits = hi[:, :LANES] + lo + hi[:, LANES:]
    lane = lax.broadcasted_iota(jnp.int32, (t, LANES), 1).astype(F32)
    neg = jnp.float32(-jnp.inf)
    logits = jnp.where(lane < N_EXPERTS, logits, neg)
    v1 = jnp.max(logits, axis=-1, keepdims=True)
    e1 = jnp.min(jnp.where(logits == v1, lane, float(LANES)), axis=-1, keepdims=True)
    rest = jnp.where(lane == e1, neg, logits)
    v2 = jnp.max(rest, axis=-1, keepdims=True)
    e2 = jnp.min(jnp.where(rest == v2, lane, float(LANES)), axis=-1, keepdims=True)
    ex = jnp.exp(v2 - v1)
    g1 = 1.0 / (1.0 + ex)
    g2 = ex / (1.0 + ex)

    onehot = jnp.where(jnp.logical_or(lane == e1, lane == e2), 1.0, 0.0)
    r_i = lax.broadcasted_iota(jnp.int32, (t, t), 0)
    c_i = lax.broadcasted_iota(jnp.int32, (t, t), 1)
    tri = jnp.where(c_i <= r_i, 1.0, 0.0).astype(BF16)
    cum = jnp.dot(tri, onehot.astype(BF16), preferred_element_type=F32)
    excl = cum - onehot + run_ref[...]
    rank1 = jnp.sum(jnp.where(lane == e1, excl, 0.0), axis=-1, keepdims=True)
    rank2 = jnp.sum(jnp.where(lane == e2, excl, 0.0), axis=-1, keepdims=True)
    run_ref[...] = run_ref[...] + cum[t - 1:t, :]

    info = jnp.where(lane == 0, e1, 0.0)
    info = jnp.where(lane == 1, e2, info)
    info = jnp.where(lane == 2, g1, info)
    info = jnp.where(lane == 3, g2, info)
    info = jnp.where(lane == 4, rank1, info)
    info = jnp.where(lane == 5, rank2, info)
    info_ref[...] = info
    info_t_ref[...] = info.T[:8, :]
    cnt_ref[...] = jnp.broadcast_to(run_ref[...], cnt_ref.shape)


def _route(h, g, router, layer):
    n = h.shape[0]
    r = jnp.zeros((D_MODEL, LANES), F32).at[:, :N_EXPERTS].set(router)
    rh = r.astype(BF16)
    rl = (r - rh.astype(F32)).astype(BF16)
    rhl = jnp.concatenate([rh, rl], axis=1)
    t = ROUTE_TILE
    return pl.pallas_call(
        _route_kernel,
        grid=(n // t,),
        in_specs=[pl.BlockSpec((t, D_MODEL), lambda i: (i, 0)),
                  pl.BlockSpec((None, 1, D_MODEL), lambda i: (layer, 0, 0)),
                  pl.BlockSpec((D_MODEL, 2 * LANES), lambda i: (0, 0))],
        out_specs=[pl.BlockSpec((t, LANES), lambda i: (i, 0)),
                   pl.BlockSpec((8, t), lambda i: (0, i)),
                   pl.BlockSpec((8, LANES), lambda i: (0, 0))],
        out_shape=[jax.ShapeDtypeStruct((n, LANES), F32),
                   jax.ShapeDtypeStruct((8, n), F32),
                   jax.ShapeDtypeStruct((8, LANES), F32)],
        scratch_shapes=[pltpu.VMEM((1, LANES), F32)],
        compiler_params=_params(("arbitrary",)),
        name="route",
    )(h, g, rhl)


def _combine_kernel(pos_ref, npos_ref, h_ref, info_ref, g_ref, ys_ref, o_ref, buf_ref, sem, *,
                    final_norm):
    t = h_ref.shape[0]
    i = pl.program_id(0)
    cur = i % 2

    def gather(idx_ref, half):
        for r in range(t):
            for k in range(TOP_K):
                pltpu.make_async_copy(ys_ref.at[pl.ds(idx_ref[k * t + r], 1)],
                                      buf_ref.at[half, k, pl.ds(r, 1)], sem.at[half]).start()

    @pl.when(i == 0)
    def _():
        gather(pos_ref, 0)

    has_next = i + 1 < pl.num_programs(0)
    for half in range(2):
        pl.when(jnp.logical_and(has_next, cur != half))(functools.partial(gather, npos_ref, half))

    for k in range(TOP_K):
        pltpu.make_async_copy(ys_ref.at[pl.ds(0, t)], buf_ref.at[cur, k], sem.at[cur]).wait()

    info = info_ref[...]
    lane = lax.broadcasted_iota(jnp.int32, info.shape, 1)
    g1 = jnp.sum(jnp.where(lane == 2, info, 0.0), axis=-1, keepdims=True)
    g2 = jnp.sum(jnp.where(lane == 3, info, 0.0), axis=-1, keepdims=True)
    y = h_ref[...] + g1 * buf_ref[cur, 0] + g2 * buf_ref[cur, 1]
    o_ref[...] = _rms_rows(y, g_ref[...]) if final_norm else y


def _combine(h, info, norm_final, ys, pos_tiles, final_norm):
    n = h.shape[0]
    t = COMBINE_TILE
    last = n // t - 1
    return pl.pallas_call(
        functools.partial(_combine_kernel, final_norm=final_norm),
        grid=(n // t,),
        in_specs=[pl.BlockSpec((TOP_K * t,), lambda i: (i,), memory_space=pltpu.SMEM),
                  pl.BlockSpec((TOP_K * t,), lambda i: (jnp.minimum(i + 1, last),),
                               memory_space=pltpu.SMEM),
                  pl.BlockSpec((t, D_MODEL), lambda i: (i, 0)),
                  pl.BlockSpec((t, LANES), lambda i: (i, 0)),
                  pl.BlockSpec((1, D_MODEL), lambda i: (0, 0)),
                  pl.BlockSpec(memory_space=pl.ANY)],
        out_specs=pl.BlockSpec((t, D_MODEL), lambda i: (i, 0)),
        out_shape=jax.ShapeDtypeStruct((n, D_MODEL), F32),
        scratch_shapes=[pltpu.VMEM((2, TOP_K, t, D_MODEL), F32), pltpu.SemaphoreType.DMA((2,))],
        compiler_params=_params(("arbitrary",)),
        name="combine",
    )(pos_tiles, pos_tiles, h, info, norm_final.reshape(1, D_MODEL), ys)


def _row_tokens_kernel(pos_ref, pad_ref, src_ref):
    i = pl.program_id(0)
    t = pos_ref.shape[0] // TOP_K
    n_ranges = pad_ref.shape[0] // 2

    @pl.when(i == 0)
    def _():
        for e in range(n_ranges):
            def zero(r, c):
                src_ref[r] = jnp.int32(0)
                return c
            lax.fori_loop(pad_ref[e], pad_ref[n_ranges + e], zero, 0)

    def body(r, c):
        for k in range(TOP_K):
            src_ref[pos_ref[k * t + r]] = i * t + r
        return c
    lax.fori_loop(0, t, body, 0, unroll=8)


def _row_tokens(pos_tiles, pad_ranges, rows):
    t = INVERT_TILE
    return pl.pallas_call(
        _row_tokens_kernel,
        grid=(pos_tiles.shape[0] // (TOP_K * t),),
        in_specs=[pl.BlockSpec((TOP_K * t,), lambda i: (i,), memory_space=pltpu.SMEM),
                  pl.BlockSpec(memory_space=pltpu.SMEM)],
        out_specs=pl.BlockSpec(memory_space=pltpu.SMEM),
        out_shape=jax.ShapeDtypeStruct((rows,), jnp.int32),
        compiler_params=_params(("arbitrary",)),
        name="row_tokens",
    )(pos_tiles, pad_ranges)


def _pos_tiles(pos1, pos2, tile):
    nt = pos1.shape[0] // tile
    return jnp.stack([pos1.reshape(nt, tile), pos2.reshape(nt, tile)], axis=1).reshape(-1)


def _moe_layer(h, norm_g, layer, router, w1, w3, w2, moe_idx, norm_final, final_norm):
    n = h.shape[0]
    info, info_t, cnt = _route(h, norm_g, router, layer)
    counts = cnt[0, :N_EXPERTS].astype(jnp.int32)
    padded = ((counts + ROW_TILE - 1) // ROW_TILE) * ROW_TILE
    ends = jnp.cumsum(padded)
    starts = ends - padded
    e1 = info_t[0].astype(jnp.int32)
    e2 = info_t[1].astype(jnp.int32)
    pos1 = starts[e1] + info_t[4].astype(jnp.int32)
    pos2 = starts[e2] + info_t[5].astype(jnp.int32)

    n_tiles = (n * TOP_K) // ROW_TILE + N_EXPERTS
    tile_start = jnp.arange(n_tiles, dtype=jnp.int32) * ROW_TILE
    in_use = tile_start < ends[-1]
    tile_expert = jnp.sum((tile_start[:, None] >= ends[None, :]).astype(jnp.int32), axis=1)
    last_expert = jnp.sum((ends[-1] - 1 >= ends).astype(jnp.int32))
    tile_expert = jnp.where(in_use, tile_expert, last_expert).astype(jnp.int32)
    tile_rows = jnp.clip((starts + counts)[tile_expert] - tile_start, 0, ROW_TILE)
    tile_valid = jnp.where(in_use, (tile_rows + SUB_ROWS - 1) // SUB_ROWS, 0).astype(jnp.int32)

    n_rows = n_tiles * ROW_TILE
    pad_ranges = jnp.concatenate([starts + counts, ends[-1:], ends,
                                  jnp.full((1,), n_rows, jnp.int32)]).astype(jnp.int32)
    src = _row_tokens(_pos_tiles(pos1, pos2, INVERT_TILE), pad_ranges, n_rows)

    ys = _ffn_experts(h, norm_g, layer, w1, w3, w2, moe_idx, tile_expert, tile_valid, src)
    return _combine(h, info, norm_final, ys, _pos_tiles(pos1, pos2, COMBINE_TILE), final_norm)


def kernel(x, positions, norm_mix, norm_ffn, norm_final, w_in, conv_w, w_ret_out, w_conv_out,
           w_out, ffn_w1, ffn_w3, ffn_w2, router, moe_w1, moe_w3, moe_w2):
    bsz, seq, _ = x.shape
    depth = w_in.shape[0]
    n = bsz * seq
    h = x.reshape(n, D_MODEL)
    norm_mix = norm_mix.reshape(depth, 1, D_MODEL)
    norm_ffn = norm_ffn.reshape(depth, 1, D_MODEL)
    cos, sin = _rope_tables(positions)
    consts = _retention_consts()
    w_ret_out, w_conv_out, w_out = (w.astype(BF16) for w in (w_ret_out, w_conv_out, w_out))
    for layer in range(depth):
        proj = _inproj(_norm_rows(h, norm_mix, layer, BF16), w_in, layer)
        og = _retention(proj, cos, sin, consts, bsz, seq)
        h = _mix_out(og, proj, h, conv_w, w_ret_out, w_conv_out, w_out, layer, seq)
        i = layer // 2
        last = layer == depth - 1
        if layer % 2 == 0:
            h = _ffn_dense(h, norm_ffn, layer, ffn_w1, ffn_w3, ffn_w2, i)
            if last:
                h = _norm_rows(h, norm_final.reshape(1, 1, D_MODEL), 0, F32)
        else:
            h = _moe_layer(h, norm_ffn, layer, router[i], moe_w1, moe_w3, moe_w2, i,
                           norm_final, last)
    return h.reshape(bsz, seq, D_MODEL)
```

```python
import functools

import jax
import jax.numpy as jnp
from jax import lax
from jax.experimental import pallas as pl
from jax.experimental.pallas import tpu as pltpu

F32 = jnp.float32
BF16 = jnp.bfloat16

D_MODEL = 2048
RET_HEADS = 4
RET_QK_DIM = 256
RET_V_DIM = 512
RET_BLOCK = 256
ROPE_BASE = 10000.0
CONV_K = 3
N_EXPERTS = 8
TOP_K = 2
EPS = 1e-6
HALF = RET_QK_DIM // 2

OFF_Q, OFF_K, OFF_V, OFF_G = 0, 1024, 2048, 4096
OFF_CB, OFF_CC, OFF_CH, OFF_GA, OFF_GB = 6144, 8192, 10240, 12288, 14336
D_IN_TOTAL = 16384

LANES = 128
VMEM_LIMIT = 56 * 1024 * 1024

ROW_TILE = 1024
INPROJ_ROWS = 2048
SUB_ROWS = 256
MIX_TILE = 256
FFN_COLS = 512
FFN_VMEM_LIMIT = 61 * 1024 * 1024
ROUTE_TILE = 512
CONV_LANES = 512
COMBINE_TILE = 256
INVERT_TILE = 2048


def _params(sem, vmem=VMEM_LIMIT):
    return pltpu.CompilerParams(dimension_semantics=sem, vmem_limit_bytes=vmem)


def _rms_rows(x, g):
    ms = jnp.mean(x * x, axis=-1, keepdims=True)
    return x * lax.rsqrt(ms + EPS) * g


def _sigmoid(x):
    return 1.0 / (1.0 + jnp.exp(-x))


def _rope_kernel(pos_ref, invf_ref, cos_ref, sin_ref):
    ang = pos_ref[...].astype(F32) * invf_ref[...]
    cos_ref[...] = jnp.cos(ang)
    sin_ref[...] = jnp.sin(ang)


def _rope_tables(positions):
    n = positions.size
    inv_freq = ROPE_BASE ** (-jnp.arange(HALF, dtype=F32) / HALF)
    rows = 2048
    return pl.pallas_call(
        _rope_kernel,
        grid=(n // rows,),
        in_specs=[pl.BlockSpec((rows, 1), lambda i: (i, 0)),
                  pl.BlockSpec((1, HALF), lambda i: (0, 0))],
        out_specs=[pl.BlockSpec((rows, HALF), lambda i: (i, 0)),
                   pl.BlockSpec((rows, HALF), lambda i: (i, 0))],
        out_shape=[jax.ShapeDtypeStruct((n, HALF), F32)] * 2,
        compiler_params=_params(("parallel",)),
        name="rope_tables",
    )(positions.reshape(n, 1), inv_freq.reshape(1, HALF))


def _norm_kernel(h_ref, g_ref, o_ref):
    o_ref[...] = _rms_rows(h_ref[...], g_ref[...]).astype(o_ref.dtype)


def _norm_rows(h, g, row, dtype):
    n = h.shape[0]
    t = 512
    return pl.pallas_call(
        _norm_kernel,
        grid=(n // t,),
        in_specs=[pl.BlockSpec((t, D_MODEL), lambda i: (i, 0)),
                  pl.BlockSpec((None, 1, D_MODEL), lambda i: (row, 0, 0))],
        out_specs=pl.BlockSpec((t, D_MODEL), lambda i: (i, 0)),
        out_shape=jax.ShapeDtypeStruct((n, D_MODEL), dtype),
        compiler_params=_params(("parallel",)),
        name="norm_rows",
    )(h, g)


def _inproj_kernel(x_ref, w_ref, o_ref, wb_ref):
    @pl.when(pl.program_id(1) == 0)
    def _():
        wb_ref[...] = w_ref[...].astype(BF16)

    for r in range(x_ref.shape[0] // ROW_TILE):
        rs = slice(r * ROW_TILE, (r + 1) * ROW_TILE)
        o_ref[rs, :] = jnp.dot(x_ref[rs, :], wb_ref[...],
                               preferred_element_type=F32).astype(o_ref.dtype)


def _inproj(xn, w_in, layer):
    n = xn.shape[0]
    tn = 1024
    tm = INPROJ_ROWS
    return pl.pallas_call(
        _inproj_kernel,
        grid=(D_IN_TOTAL // tn, n // tm),
        in_specs=[pl.BlockSpec((tm, D_MODEL), lambda j, i: (i, 0)),
                  pl.BlockSpec((None, D_MODEL, tn), lambda j, i: (layer, 0, j))],
        out_specs=pl.BlockSpec((tm, tn), lambda j, i: (i, j)),
        out_shape=jax.ShapeDtypeStruct((n, D_IN_TOTAL), BF16),
        scratch_shapes=[pltpu.VMEM((D_MODEL, tn), BF16)],
        compiler_params=_params(("parallel", "arbitrary")),
        name="inproj",
    )(xn, w_in)


def _retention_kernel(gc_ref, q_ref, k_ref, v_ref, g_ref, cos_ref, sin_ref, dm_ref, xi_ref,
                      zeta_ref, o_ref, qr_ref, qx_ref, kr_ref, kz_ref):
    seq = q_ref.shape[0]
    scale = RET_QK_DIM ** -0.5

    def rot(r, c):
        sl = pl.ds(pl.multiple_of(r * RET_BLOCK, RET_BLOCK), RET_BLOCK)
        cos = cos_ref[sl, :]
        sin = sin_ref[sl, :]
        xi = xi_ref[...]
        zeta = zeta_ref[...]
        q = q_ref[sl, :].astype(F32)
        q1, q2 = q[:, :HALF], q[:, HALF:]
        qa = (q1 * cos - q2 * sin) * scale
        qb = (q1 * sin + q2 * cos) * scale
        qr_ref[sl, :HALF] = qa.astype(BF16)
        qr_ref[sl, HALF:] = qb.astype(BF16)
        qx_ref[sl, :HALF] = (qa * xi).astype(BF16)
        qx_ref[sl, HALF:] = (qb * xi).astype(BF16)
        k = k_ref[sl, :].astype(F32)
        k1, k2 = k[:, :HALF], k[:, HALF:]
        ka = k1 * cos - k2 * sin
        kb = k1 * sin + k2 * cos
        kr_ref[sl, :HALF] = ka.astype(BF16)
        kr_ref[sl, HALF:] = kb.astype(BF16)
        kz_ref[sl, :HALF] = (ka * zeta).astype(BF16)
        kz_ref[sl, HALF:] = (kb * zeta).astype(BF16)
        return c
    lax.fori_loop(0, seq // RET_BLOCK, rot, 0)

    gamma_block = gc_ref[pl.program_id(1)]
    n_blocks = seq // RET_BLOCK
    state = None
    for c in range(n_blocks):
        sl = slice(c * RET_BLOCK, (c + 1) * RET_BLOCK)
        vc = v_ref[sl, :]
        scores = lax.dot_general(qr_ref[sl, :], kr_ref[sl, :], (((1,), (1,)), ((), ())),
                                 preferred_element_type=F32) * dm_ref[...]
        o = jnp.dot(scores.astype(BF16), vc, preferred_element_type=F32)
        if state is not None:
            o = o + jnp.dot(qx_ref[sl, :], state.astype(BF16), preferred_element_type=F32)
        if c + 1 < n_blocks:
            new = lax.dot_general(kz_ref[sl, :], vc, (((0,), (0,)), ((), ())),
                                  preferred_element_type=F32)
            state = new if state is None else state * gamma_block + new
        mu = jnp.mean(o, axis=-1, keepdims=True)
        d = o - mu
        var = jnp.mean(d * d, axis=-1, keepdims=True)
        on = d * lax.rsqrt(var + EPS)
        g = g_ref[sl, :].astype(F32)
        o_ref[sl, :] = (on * (g * _sigmoid(g))).astype(o_ref.dtype)


def _retention_consts():
    h = RET_HEADS
    log_gamma = jnp.log1p(-jnp.power(2.0, -5.0 - jnp.arange(h, dtype=F32)))
    idx = jnp.arange(RET_BLOCK, dtype=F32)
    diff = idx[:, None] - idx[None, :]
    causal = diff >= 0
    decay_mask = jnp.where(causal[None],
                           jnp.exp(jnp.where(causal, diff, 0.0)[None] * log_gamma[:, None, None]),
                           0.0)
    xi = jnp.exp((idx + 1.0)[None] * log_gamma[:, None])
    zeta = jnp.exp((RET_BLOCK - 1.0 - idx)[None] * log_gamma[:, None])
    gamma_block = jnp.exp(RET_BLOCK * log_gamma)
    xi_b = jnp.broadcast_to(xi[:, :, None], (h, RET_BLOCK, HALF))
    zeta_b = jnp.broadcast_to(zeta[:, :, None], (h, RET_BLOCK, HALF))
    return decay_mask, xi_b, zeta_b, gamma_block


def _retention(proj, cos, sin, consts, bsz, seq):
    decay_mask, xi_b, zeta_b, gamma_chunk = consts
    n = proj.shape[0]
    qb, vb = RET_QK_DIM, RET_V_DIM
    return pl.pallas_call(
        _retention_kernel,
        grid=(bsz, RET_HEADS),
        in_specs=[
            pl.BlockSpec(memory_space=pltpu.SMEM),
            pl.BlockSpec((seq, qb), lambda b, h: (b, OFF_Q // qb + h)),
            pl.BlockSpec((seq, qb), lambda b, h: (b, OFF_K // qb + h)),
            pl.BlockSpec((seq, vb), lambda b, h: (b, OFF_V // vb + h)),
            pl.BlockSpec((seq, vb), lambda b, h: (b, OFF_G // vb + h)),
            pl.BlockSpec((seq, HALF), lambda b, h: (b, 0)),
            pl.BlockSpec((seq, HALF), lambda b, h: (b, 0)),
            pl.BlockSpec((None, RET_BLOCK, RET_BLOCK), lambda b, h: (h, 0, 0)),
            pl.BlockSpec((None, RET_BLOCK, HALF), lambda b, h: (h, 0, 0)),
            pl.BlockSpec((None, RET_BLOCK, HALF), lambda b, h: (h, 0, 0)),
        ],
        out_specs=pl.BlockSpec((seq, vb), lambda b, h: (b, h)),
        scratch_shapes=[pltpu.VMEM((seq, qb), BF16), pltpu.VMEM((seq, qb), BF16),
                        pltpu.VMEM((seq, qb), BF16), pltpu.VMEM((seq, qb), BF16)],
        out_shape=jax.ShapeDtypeStruct((n, RET_HEADS * vb), BF16),
        compiler_params=_params(("parallel", "parallel")),
        name="retention",
    )(gamma_chunk, proj, proj, proj, proj, cos, sin, decay_mask, xi_b, zeta_b)


def _mix_out_kernel(og_ref, cb_ref, cc_ref, ch_ref, ga_ref, gb_ref, h_ref, cw_ref,
                    wr_ref, wc_ref, wo_ref, o_ref, tail_ref, *, tiles_per_seq):
    t = og_ref.shape[0]

    @pl.when(pl.program_id(0) % tiles_per_seq == 0)
    def _():
        tail_ref[...] = jnp.zeros_like(tail_ref)

    yr = jnp.dot(og_ref[...], wr_ref[...], preferred_element_type=F32)
    row = lax.broadcasted_iota(jnp.int32, (t, CONV_LANES), 0)
    yc = None
    for c in range(D_MODEL // CONV_LANES):
        cs = slice(c * CONV_LANES, (c + 1) * CONV_LANES)
        u = cc_ref[:, cs].astype(F32) * ch_ref[:, cs].astype(F32)
        prev1 = tail_ref[7:8, cs]
        prev2 = tail_ref[6:7, cs]
        u1 = jnp.where(row >= 1, pltpu.roll(u, 1, 0), prev1)
        u2 = jnp.where(row >= 2, pltpu.roll(u, 2, 0), jnp.where(row == 1, prev1, prev2))
        w = cw_ref[:, cs]
        y = w[0:1, :] * u2 + w[1:2, :] * u1 + w[2:3, :] * u
        cv = (cb_ref[:, cs].astype(F32) * y).astype(BF16)
        tail_ref[:, cs] = u[t - 8:t, :]
        part = jnp.dot(cv, wc_ref[cs, :], preferred_element_type=F32)
        yc = part if yc is None else yc + part
    ga = ga_ref[...].astype(F32)
    gb = gb_ref[...].astype(F32)
    merged = (_sigmoid(ga) * yr + _sigmoid(gb) * yc).astype(BF16)
    o_ref[...] = h_ref[...] + jnp.dot(merged, wo_ref[...], preferred_element_type=F32)


def _mix_out(og, proj, h, conv_w, w_ret_out, w_conv_out, w_out, layer, seq):
    n = h.shape[0]
    t = MIX_TILE
    rows = lambda i: (i, 0)
    pcol = lambda off: pl.BlockSpec((t, D_MODEL), lambda i: (i, off // D_MODEL))
    wspec = pl.BlockSpec((None, D_MODEL, D_MODEL), lambda i: (layer, 0, 0),
                         pipeline_mode=pl.Buffered(1))
    return pl.pallas_call(
        functools.partial(_mix_out_kernel, tiles_per_seq=seq // t),
        grid=(n // t,),
        in_specs=[pl.BlockSpec((t, D_MODEL), rows),
                  pcol(OFF_CB), pcol(OFF_CC), pcol(OFF_CH), pcol(OFF_GA), pcol(OFF_GB),
                  pl.BlockSpec((t, D_MODEL), rows),
                  pl.BlockSpec((None, CONV_K, D_MODEL), lambda i: (layer, 0, 0)),
                  wspec, wspec, wspec],
        out_specs=pl.BlockSpec((t, D_MODEL), rows),
        out_shape=jax.ShapeDtypeStruct((n, D_MODEL), F32),
        scratch_shapes=[pltpu.VMEM((8, D_MODEL), F32)],
        compiler_params=_params(("arbitrary",)),
        name="mix_out",
    )(og, proj, proj, proj, proj, proj, h, conv_w, w_ret_out, w_conv_out, w_out)


def _normalise_tile(src_ref, g_ref, xb_ref, o_ref, residual):
    def body(r, c):
        sl = pl.ds(pl.multiple_of(r * SUB_ROWS, SUB_ROWS), SUB_ROWS)
        x = src_ref[sl, :]
        xb_ref[sl, :] = _rms_rows(x, g_ref[...]).astype(BF16)
        o_ref[sl, :] = x if residual else jnp.zeros_like(x)
        return c
    lax.fori_loop(0, ROW_TILE // SUB_ROWS, body, 0)


def _swiglu_rows(xb_ref, w1_ref, w3_ref, w2_ref, o_ref, rows):
    x = xb_ref[:rows, :]
    a = jnp.dot(x, w1_ref[...].astype(BF16), preferred_element_type=F32)
    b = jnp.dot(x, w3_ref[...].astype(BF16), preferred_element_type=F32)
    hh = (a * _sigmoid(a) * b).astype(BF16)
    o_ref[:rows, :] += jnp.dot(hh, w2_ref[...].astype(BF16), preferred_element_type=F32)


def _ffn_dense_kernel(x_ref, g_ref, w1_ref, w3_ref, w2_ref, o_ref, xb_ref):
    @pl.when(pl.program_id(1) == 0)
    def _():
        _normalise_tile(x_ref, g_ref, xb_ref, o_ref, True)

    _swiglu_rows(xb_ref, w1_ref, w3_ref, w2_ref, o_ref, ROW_TILE)


def _ffn_dense(x, g, g_row, w1, w3, w2, w_idx):
    rows = x.shape[0]
    tf = FFN_COLS
    nf = w1.shape[-1] // tf
    return pl.pallas_call(
        _ffn_dense_kernel,
        grid=(rows // ROW_TILE, nf),
        in_specs=[
            pl.BlockSpec((ROW_TILE, D_MODEL), lambda i, f: (i, 0), pipeline_mode=pl.Buffered(1)),
            pl.BlockSpec((None, 1, D_MODEL), lambda i, f: (g_row, 0, 0)),
            pl.BlockSpec((None, D_MODEL, tf), lambda i, f: (w_idx, 0, f)),
            pl.BlockSpec((None, D_MODEL, tf), lambda i, f: (w_idx, 0, f)),
            pl.BlockSpec((None, tf, D_MODEL), lambda i, f: (w_idx, f, 0)),
        ],
        out_specs=pl.BlockSpec((ROW_TILE, D_MODEL), lambda i, f: (i, 0)),
        scratch_shapes=[pltpu.VMEM((ROW_TILE, D_MODEL), BF16)],
        out_shape=jax.ShapeDtypeStruct((rows, D_MODEL), F32),
        compiler_params=_params(("parallel", "arbitrary"), FFN_VMEM_LIMIT),
        name="ffn_dense",
    )(x, g, w1, w3, w2)


def _experts_kernel(te_ref, tv_ref, src_ref, nsrc_ref, h_hbm, g_ref, w1_ref, w3_ref, w2_ref,
                    o_ref, land_ref, xb_ref, sem, *, rows_per_step):
    i = pl.program_id(0)
    f = pl.program_id(1)
    land_rows = land_ref.shape[0]

    def row_copy(idx_ref, r):
        tok = idx_ref[jnp.minimum(r, ROW_TILE - 1)]
        return pltpu.make_async_copy(h_hbm.at[pl.ds(tok, 1)], land_ref.at[pl.ds(r, 1)], sem)

    def wait_landing():
        pltpu.make_async_copy(h_hbm.at[pl.ds(0, land_rows)], land_ref, sem).wait()

    @pl.when(jnp.logical_and(i == 0, f == 0))
    def _():
        def body(r, c):
            row_copy(src_ref, r).start()
            return c
        lax.fori_loop(0, land_rows, body, 0)
        wait_landing()

    @pl.when(f == 0)
    def _():
        _normalise_tile(land_ref, g_ref, xb_ref, o_ref, False)

    def step(groups):
        base = f * rows_per_step
        for j in range(rows_per_step):
            row_copy(nsrc_ref, base + j).start()
        _swiglu_rows(xb_ref, w1_ref, w3_ref, w2_ref, o_ref, groups * SUB_ROWS)

    for groups in range(1, ROW_TILE // SUB_ROWS + 1):
        pl.when(tv_ref[i] == groups)(functools.partial(step, groups))

    @pl.when(jnp.logical_and(f == pl.num_programs(1) - 1, tv_ref[i] > 0))
    def _():
        wait_landing()


def _ffn_experts(h, g, g_row, w1, w3, w2, w_idx, tile_expert, tile_valid, src):
    n_tiles = tile_expert.shape[0]
    tf = FFN_COLS
    nf = w1.shape[-1] // tf
    rows_per_step = -(-ROW_TILE // nf)
    while (rows_per_step * nf) % 8:
        rows_per_step += 1
    land_rows = rows_per_step * nf

    def wcol(i, f, te, tv):
        return jnp.where(tv[i] > 0, f, nf - 1)

    return pl.pallas_call(
        functools.partial(_experts_kernel, rows_per_step=rows_per_step),
        grid_spec=pltpu.PrefetchScalarGridSpec(
            num_scalar_prefetch=2,
            grid=(n_tiles, nf),
            in_specs=[
                pl.BlockSpec((ROW_TILE,), lambda i, f, te, tv: (i,), memory_space=pltpu.SMEM),
                pl.BlockSpec((ROW_TILE,), lambda i, f, te, tv: (jnp.minimum(i + 1, n_tiles - 1),),
                             memory_space=pltpu.SMEM),
                pl.BlockSpec(memory_space=pl.ANY),
                pl.BlockSpec((None, 1, D_MODEL), lambda i, f, te, tv: (g_row, 0, 0)),
                pl.BlockSpec((None, None, D_MODEL, tf),
                             lambda i, f, te, tv: (w_idx, te[i], 0, wcol(i, f, te, tv))),
                pl.BlockSpec((None, None, D_MODEL, tf),
                             lambda i, f, te, tv: (w_idx, te[i], 0, wcol(i, f, te, tv))),
                pl.BlockSpec((None, None, tf, D_MODEL),
                             lambda i, f, te, tv: (w_idx, te[i], wcol(i, f, te, tv), 0)),
            ],
            out_specs=pl.BlockSpec((ROW_TILE, D_MODEL), lambda i, f, te, tv: (i, 0)),
            scratch_shapes=[pltpu.VMEM((land_rows, D_MODEL), F32),
                            pltpu.VMEM((ROW_TILE, D_MODEL), BF16),
                            pltpu.SemaphoreType.DMA(())],
        ),
        out_shape=jax.ShapeDtypeStruct((n_tiles * ROW_TILE, D_MODEL), F32),
        compiler_params=_params(("arbitrary", "arbitrary"), FFN_VMEM_LIMIT),
        name="ffn_experts",
    )(tile_expert, tile_valid, src, src, h, g, w1, w3, w2)


def _route_kernel(h_ref, g_ref, rhl_ref, info_ref, info_t_ref, cnt_ref, run_ref):
    t = h_ref.shape[0]

    @pl.when(pl.program_id(0) == 0)
    def _():
        run_ref[...] = jnp.zeros_like(run_ref)

    xn = _rms_rows(h_ref[...], g_ref[...])
    xh = xn.astype(BF16)
    xl = (xn - xh.astype(F32)).astype(BF16)
    hi = jnp.dot(xh, rhl_ref[...], preferred_element_type=F32)
    lo = jnp.dot(xl, rhl_ref[:, :LANES], preferred_element_type=F32)
    logits = hi[:, :LANES] + lo + hi[:, LANES:]
    lane = lax.broadcasted_iota(jnp.int32, (t, LANES), 1).astype(F32)
    neg = jnp.float32(-jnp.inf)
    logits = jnp.where(lane < N_EXPERTS, logits, neg)
    v1 = jnp.max(logits, axis=-1, keepdims=True)
    e1 = jnp.min(jnp.where(logits == v1, lane, float(LANES)), axis=-1, keepdims=True)
    rest = jnp.where(lane == e1, neg, logits)
    v2 = jnp.max(rest, axis=-1, keepdims=True)
    e2 = jnp.min(jnp.where(rest == v2, lane, float(LANES)), axis=-1, keepdims=True)
    ex = jnp.exp(v2 - v1)
    g1 = 1.0 / (1.0 + ex)
    g2 = ex / (1.0 + ex)

    onehot = jnp.where(jnp.logical_or(lane == e1, lane == e2), 1.0, 0.0)
    r_i = lax.broadcasted_iota(jnp.int32, (t, t), 0)
    c_i = lax.broadcasted_iota(jnp.int32, (t, t), 1)
    tri = jnp.where(c_i <= r_i, 1.0, 0.0).astype(BF16)
    cum = jnp.dot(tri, onehot.astype(BF16), preferred_element_type=F32)
    excl = cum - onehot + run_ref[...]
    rank1 = jnp.sum(jnp.where(lane == e1, excl, 0.0), axis=-1, keepdims=True)
    rank2 = jnp.sum(jnp.where(lane == e2, excl, 0.0), axis=-1, keepdims=True)
    run_ref[...] = run_ref[...] + cum[t - 1:t, :]

    info = jnp.where(lane == 0, e1, 0.0)
    info = jnp.where(lane == 1, e2, info)
    info = jnp.where(lane == 2, g1, info)
    info = jnp.where(lane == 3, g2, info)
    info = jnp.where(lane == 4, rank1, info)
    info = jnp.where(lane == 5, rank2, info)
    info_ref[...] = info
    info_t_ref[...] = info.T[:8, :]
    cnt_ref[...] = jnp.broadcast_to(run_ref[...], cnt_ref.shape)


def _route(h, g, router, layer):
    n = h.shape[0]
    r = jnp.zeros((D_MODEL, LANES), F32).at[:, :N_EXPERTS].set(router)
    rh = r.astype(BF16)
    rl = (r - rh.astype(F32)).astype(BF16)
    rhl = jnp.concatenate([rh, rl], axis=1)
    t = ROUTE_TILE
    return pl.pallas_call(
        _route_kernel,
        grid=(n // t,),
        in_specs=[pl.BlockSpec((t, D_MODEL), lambda i: (i, 0)),
                  pl.BlockSpec((None, 1, D_MODEL), lambda i: (layer, 0, 0)),
                  pl.BlockSpec((D_MODEL, 2 * LANES), lambda i: (0, 0))],
        out_specs=[pl.BlockSpec((t, LANES), lambda i: (i, 0)),
                   pl.BlockSpec((8, t), lambda i: (0, i)),
                   pl.BlockSpec((8, LANES), lambda i: (0, 0))],
        out_shape=[jax.ShapeDtypeStruct((n, LANES), F32),
                   jax.ShapeDtypeStruct((8, n), F32),
                   jax.ShapeDtypeStruct((8, LANES), F32)],
        scratch_shapes=[pltpu.VMEM((1, LANES), F32)],
        compiler_params=_params(("arbitrary",)),
        name="route",
    )(h, g, rhl)


def _combine_kernel(pos_ref, npos_ref, h_ref, info_ref, g_ref, ys_ref, o_ref, buf_ref, sem, *,
                    final_norm):
    t = h_ref.shape[0]
    i = pl.program_id(0)
    cur = i % 2

    def gather(idx_ref, half):
        for r in range(t):
            for k in range(TOP_K):
                pltpu.make_async_copy(ys_ref.at[pl.ds(idx_ref[k * t + r], 1)],
                                      buf_ref.at[half, k, pl.ds(r, 1)], sem.at[half]).start()

    @pl.when(i == 0)
    def _():
        gather(pos_ref, 0)

    has_next = i + 1 < pl.num_programs(0)
    for half in range(2):
        pl.when(jnp.logical_and(has_next, cur != half))(functools.partial(gather, npos_ref, half))

    for k in range(TOP_K):
        pltpu.make_async_copy(ys_ref.at[pl.ds(0, t)], buf_ref.at[cur, k], sem.at[cur]).wait()

    info = info_ref[...]
    lane = lax.broadcasted_iota(jnp.int32, info.shape, 1)
    g1 = jnp.sum(jnp.where(lane == 2, info, 0.0), axis=-1, keepdims=True)
    g2 = jnp.sum(jnp.where(lane == 3, info, 0.0), axis=-1, keepdims=True)
    y = h_ref[...] + g1 * buf_ref[cur, 0] + g2 * buf_ref[cur, 1]
    o_ref[...] = _rms_rows(y, g_ref[...]) if final_norm else y


def _combine(h, info, norm_final, ys, pos_tiles, final_norm):
    n = h.shape[0]
    t = COMBINE_TILE
    last = n // t - 1
    return pl.pallas_call(
        functools.partial(_combine_kernel, final_norm=final_norm),
        grid=(n // t,),
        in_specs=[pl.BlockSpec((TOP_K * t,), lambda i: (i,), memory_space=pltpu.SMEM),
                  pl.BlockSpec((TOP_K * t,), lambda i: (jnp.minimum(i + 1, last),),
                               memory_space=pltpu.SMEM),
                  pl.BlockSpec((t, D_MODEL), lambda i: (i, 0)),
                  pl.BlockSpec((t, LANES), lambda i: (i, 0)),
                  pl.BlockSpec((1, D_MODEL), lambda i: (0, 0)),
                  pl.BlockSpec(memory_space=pl.ANY)],
        out_specs=pl.BlockSpec((t, D_MODEL), lambda i: (i, 0)),
        out_shape=jax.ShapeDtypeStruct((n, D_MODEL), F32),
        scratch_shapes=[pltpu.VMEM((2, TOP_K, t, D_MODEL), F32), pltpu.SemaphoreType.DMA((2,))],
        compiler_params=_params(("arbitrary",)),
        name="combine",
    )(pos_tiles, pos_tiles, h, info, norm_final.reshape(1, D_MODEL), ys)


def _row_tokens_kernel(pos_ref, pad_ref, src_ref):
    i = pl.program_id(0)
    t = pos_ref.shape[0] // TOP_K
    n_ranges = pad_ref.shape[0] // 2

    @pl.when(i == 0)
    def _():
        for e in range(n_ranges):
            def zero(r, c):
                src_ref[r] = jnp.int32(0)
                return c
            lax.fori_loop(pad_ref[e], pad_ref[n_ranges + e], zero, 0)

    def body(r, c):
        for k in range(TOP_K):
            src_ref[pos_ref[k * t + r]] = i * t + r
        return c
    lax.fori_loop(0, t, body, 0, unroll=8)


def _row_tokens(pos_tiles, pad_ranges, rows):
    t = INVERT_TILE
    return pl.pallas_call(
        _row_tokens_kernel,
        grid=(pos_tiles.shape[0] // (TOP_K * t),),
        in_specs=[pl.BlockSpec((TOP_K * t,), lambda i: (i,), memory_space=pltpu.SMEM),
                  pl.BlockSpec(memory_space=pltpu.SMEM)],
        out_specs=pl.BlockSpec(memory_space=pltpu.SMEM),
        out_shape=jax.ShapeDtypeStruct((rows,), jnp.int32),
        compiler_params=_params(("arbitrary",)),
        name="row_tokens",
    )(pos_tiles, pad_ranges)


def _pos_tiles(pos1, pos2, tile):
    nt = pos1.shape[0] // tile
    return jnp.stack([pos1.reshape(nt, tile), pos2.reshape(nt, tile)], axis=1).reshape(-1)


def _moe_layer(h, norm_g, layer, router, w1, w3, w2, moe_idx, norm_final, final_norm):
    n = h.shape[0]
    info, info_t, cnt = _route(h, norm_g, router, layer)
    counts = cnt[0, :N_EXPERTS].astype(jnp.int32)
    padded = ((counts + ROW_TILE - 1) // ROW_TILE) * ROW_TILE
    ends = jnp.cumsum(padded)
    starts = ends - padded
    e1 = info_t[0].astype(jnp.int32)
    e2 = info_t[1].astype(jnp.int32)
    pos1 = starts[e1] + info_t[4].astype(jnp.int32)
    pos2 = starts[e2] + info_t[5].astype(jnp.int32)

    n_tiles = (n * TOP_K) // ROW_TILE + N_EXPERTS
    tile_start = jnp.arange(n_tiles, dtype=jnp.int32) * ROW_TILE
    in_use = tile_start < ends[-1]
    tile_expert = jnp.sum((tile_start[:, None] >= ends[None, :]).astype(jnp.int32), axis=1)
    last_expert = jnp.sum((ends[-1] - 1 >= ends).astype(jnp.int32))
    tile_expert = jnp.where(in_use, tile_expert, last_expert).astype(jnp.int32)
    tile_rows = jnp.clip((starts + counts)[tile_expert] - tile_start, 0, ROW_TILE)
    tile_valid = jnp.where(in_use, (tile_rows + SUB_ROWS - 1) // SUB_ROWS, 0).astype(jnp.int32)

    n_rows = n_tiles * ROW_TILE
    pad_ranges = jnp.concatenate([starts + counts, ends[-1:], ends,
                                  jnp.full((1,), n_rows, jnp.int32)]).astype(jnp.int32)
    src = _row_tokens(_pos_tiles(pos1, pos2, INVERT_TILE), pad_ranges, n_rows)

    ys = _ffn_experts(h, norm_g, layer, w1, w3, w2, moe_idx, tile_expert, tile_valid, src)
    return _combine(h, info, norm_final, ys, _pos_tiles(pos1, pos2, COMBINE_TILE), final_norm)


def kernel(x, positions, norm_mix, norm_ffn, norm_final, w_in, conv_w, w_ret_out, w_conv_out,
           w_out, ffn_w1, ffn_w3, ffn_w2, router, moe_w1, moe_w3, moe_w2):
    bsz, seq, _ = x.shape
    depth = w_in.shape[0]
    n = bsz * seq
    h = x.reshape(n, D_MODEL)
    norm_mix = norm_mix.reshape(depth, 1, D_MODEL)
    norm_ffn = norm_ffn.reshape(depth, 1, D_MODEL)
    cos, sin = _rope_tables(positions)
    consts = _retention_consts()
    w_ret_out, w_conv_out, w_out = (w.astype(BF16) for w in (w_ret_out, w_conv_out, w_out))
    for layer in range(depth):
        proj = _inproj(_norm_rows(h, norm_mix, layer, BF16), w_in, layer)
        og = _retention(proj, cos, sin, consts, bsz, seq)
        h = _mix_out(og, proj, h, conv_w, w_ret_out, w_conv_out, w_out, layer, seq)
        i = layer // 2
        last = layer == depth - 1
        if layer % 2 == 0:
            h = _ffn_dense(h, norm_ffn, layer, ffn_w1, ffn_w3, ffn_w2, i)
            if last:
                h = _norm_rows(h, norm_final.reshape(1, 1, D_MODEL), 0, F32)
        else:
            h = _moe_layer(h, norm_ffn, layer, router[i], moe_w1, moe_w3, moe_w2, i,
                           norm_final, last)
    return h.reshape(bsz, seq, D_MODEL)
```

```python
import functools

import jax
import jax.numpy as jnp
from jax import lax
from jax.experimental import pallas as pl
from jax.experimental.pallas import tpu as pltpu

F32 = jnp.float32
BF16 = jnp.bfloat16

D_MODEL = 2048
RET_HEADS = 4
RET_QK_DIM = 256
RET_V_DIM = 512
RET_BLOCK = 256
ROPE_BASE = 10000.0
CONV_K = 3
N_EXPERTS = 8
TOP_K = 2
EPS = 1e-6
HALF = RET_QK_DIM // 2

OFF_Q, OFF_K, OFF_V, OFF_G = 0, 1024, 2048, 4096
OFF_CB, OFF_CC, OFF_CH, OFF_GA, OFF_GB = 6144, 8192, 10240, 12288, 14336
D_IN_TOTAL = 16384

LANES = 128
SUBLANES = 8
VMEM_LIMIT = 56 * 1024 * 1024

ROW_TILE = 1024
ROPE_ROWS = 2048
NORM_ROWS = 512
INPROJ_ROWS = 2048
INPROJ_COLS = 1024
SUB_ROWS = 256
MIX_TILE = 256
FFN_COLS = 512
FFN_VMEM_LIMIT = 61 * 1024 * 1024
ROUTE_TILE = 512
CONV_LANES = 512
COMBINE_TILE = 256
INVERT_TILE = 2048


def _params(sem, vmem=VMEM_LIMIT):
    return pltpu.CompilerParams(dimension_semantics=sem, vmem_limit_bytes=vmem)


def _rms_rows(x, g):
    ms = jnp.mean(x * x, axis=-1, keepdims=True)
    return x * lax.rsqrt(ms + EPS) * g


def _sigmoid(x):
    return 1.0 / (1.0 + jnp.exp(-x))


def _rope_kernel(pos_ref, invf_ref, cos_ref, sin_ref):
    ang = pos_ref[...].astype(F32) * invf_ref[...]
    cos_ref[...] = jnp.cos(ang)
    sin_ref[...] = jnp.sin(ang)


def _rope_tables(positions):
    n = positions.size
    inv_freq = ROPE_BASE ** (-jnp.arange(HALF, dtype=F32) / HALF)
    rows = ROPE_ROWS
    return pl.pallas_call(
        _rope_kernel,
        grid=(n // rows,),
        in_specs=[pl.BlockSpec((rows, 1), lambda i: (i, 0)),
                  pl.BlockSpec((1, HALF), lambda i: (0, 0))],
        out_specs=[pl.BlockSpec((rows, HALF), lambda i: (i, 0)),
                   pl.BlockSpec((rows, HALF), lambda i: (i, 0))],
        out_shape=[jax.ShapeDtypeStruct((n, HALF), F32)] * 2,
        compiler_params=_params(("parallel",)),
        name="rope_tables",
    )(positions.reshape(n, 1), inv_freq.reshape(1, HALF))


def _norm_kernel(h_ref, g_ref, o_ref):
    o_ref[...] = _rms_rows(h_ref[...], g_ref[...]).astype(o_ref.dtype)


def _norm_rows(h, g, row, dtype):
    n = h.shape[0]
    t = NORM_ROWS
    return pl.pallas_call(
        _norm_kernel,
        grid=(n // t,),
        in_specs=[pl.BlockSpec((t, D_MODEL), lambda i: (i, 0)),
                  pl.BlockSpec((None, 1, D_MODEL), lambda i: (row, 0, 0))],
        out_specs=pl.BlockSpec((t, D_MODEL), lambda i: (i, 0)),
        out_shape=jax.ShapeDtypeStruct((n, D_MODEL), dtype),
        compiler_params=_params(("parallel",)),
        name="norm_rows",
    )(h, g)


def _inproj_kernel(x_ref, w_ref, o_ref, wb_ref):
    @pl.when(pl.program_id(1) == 0)
    def _():
        wb_ref[...] = w_ref[...].astype(BF16)

    for r in range(x_ref.shape[0] // ROW_TILE):
        rs = slice(r * ROW_TILE, (r + 1) * ROW_TILE)
        o_ref[rs, :] = jnp.dot(x_ref[rs, :], wb_ref[...],
                               preferred_element_type=F32).astype(o_ref.dtype)


def _inproj(xn, w_in, layer):
    n = xn.shape[0]
    tn = INPROJ_COLS
    tm = INPROJ_ROWS
    return pl.pallas_call(
        _inproj_kernel,
        grid=(D_IN_TOTAL // tn, n // tm),
        in_specs=[pl.BlockSpec((tm, D_MODEL), lambda j, i: (i, 0)),
                  pl.BlockSpec((None, D_MODEL, tn), lambda j, i: (layer, 0, j))],
        out_specs=pl.BlockSpec((tm, tn), lambda j, i: (i, j)),
        out_shape=jax.ShapeDtypeStruct((n, D_IN_TOTAL), BF16),
        scratch_shapes=[pltpu.VMEM((D_MODEL, tn), BF16)],
        compiler_params=_params(("parallel", "arbitrary")),
        name="inproj",
    )(xn, w_in)


def _retention_kernel(gc_ref, q_ref, k_ref, v_ref, g_ref, cos_ref, sin_ref, dm_ref, xi_ref,
                      zeta_ref, o_ref, qr_ref, qx_ref, kr_ref, kz_ref):
    seq = q_ref.shape[0]
    scale = RET_QK_DIM ** -0.5

    def rot(r, c):
        sl = pl.ds(pl.multiple_of(r * RET_BLOCK, RET_BLOCK), RET_BLOCK)
        cos = cos_ref[sl, :]
        sin = sin_ref[sl, :]
        xi = xi_ref[...]
        zeta = zeta_ref[...]
        q = q_ref[sl, :].astype(F32)
        q1, q2 = q[:, :HALF], q[:, HALF:]
        qa = (q1 * cos - q2 * sin) * scale
        qb = (q1 * sin + q2 * cos) * scale
        qr_ref[sl, :HALF] = qa.astype(BF16)
        qr_ref[sl, HALF:] = qb.astype(BF16)
        qx_ref[sl, :HALF] = (qa * xi).astype(BF16)
        qx_ref[sl, HALF:] = (qb * xi).astype(BF16)
        k = k_ref[sl, :].astype(F32)
        k1, k2 = k[:, :HALF], k[:, HALF:]
        ka = k1 * cos - k2 * sin
        kb = k1 * sin + k2 * cos
        kr_ref[sl, :HALF] = ka.astype(BF16)
        kr_ref[sl, HALF:] = kb.astype(BF16)
        kz_ref[sl, :HALF] = (ka * zeta).astype(BF16)
        kz_ref[sl, HALF:] = (kb * zeta).astype(BF16)
        return c
    lax.fori_loop(0, seq // RET_BLOCK, rot, 0)

    gamma_block = gc_ref[pl.program_id(1)]
    n_blocks = seq // RET_BLOCK
    state = None
    for c in range(n_blocks):
        sl = slice(c * RET_BLOCK, (c + 1) * RET_BLOCK)
        vc = v_ref[sl, :]
        scores = lax.dot_general(qr_ref[sl, :], kr_ref[sl, :], (((1,), (1,)), ((), ())),
                                 preferred_element_type=F32) * dm_ref[...]
        o = jnp.dot(scores.astype(BF16), vc, preferred_element_type=F32)
        if state is not None:
            o = o + jnp.dot(qx_ref[sl, :], state.astype(BF16), preferred_element_type=F32)
        if c + 1 < n_blocks:
            new = lax.dot_general(kz_ref[sl, :], vc, (((0,), (0,)), ((), ())),
                                  preferred_element_type=F32)
            state = new if state is None else state * gamma_block + new
        mu = jnp.mean(o, axis=-1, keepdims=True)
        d = o - mu
        var = jnp.mean(d * d, axis=-1, keepdims=True)
        on = d * lax.rsqrt(var + EPS)
        g = g_ref[sl, :].astype(F32)
        o_ref[sl, :] = (on * (g * _sigmoid(g))).astype(o_ref.dtype)


def _retention_consts():
    h = RET_HEADS
    log_gamma = jnp.log1p(-jnp.power(2.0, -5.0 - jnp.arange(h, dtype=F32)))
    idx = jnp.arange(RET_BLOCK, dtype=F32)
    diff = idx[:, None] - idx[None, :]
    causal = diff >= 0
    decay_mask = jnp.where(causal[None],
                           jnp.exp(jnp.where(causal, diff, 0.0)[None] * log_gamma[:, None, None]),
                           0.0)
    xi = jnp.exp((idx + 1.0)[None] * log_gamma[:, None])
    zeta = jnp.exp((RET_BLOCK - 1.0 - idx)[None] * log_gamma[:, None])
    gamma_block = jnp.exp(RET_BLOCK * log_gamma)
    xi_b = jnp.broadcast_to(xi[:, :, None], (h, RET_BLOCK, HALF))
    zeta_b = jnp.broadcast_to(zeta[:, :, None], (h, RET_BLOCK, HALF))
    return decay_mask, xi_b, zeta_b, gamma_block


def _retention(proj, cos, sin, consts, bsz, seq):
    decay_mask, xi_b, zeta_b, gamma_chunk = consts
    n = proj.shape[0]
    qb, vb = RET_QK_DIM, RET_V_DIM
    return pl.pallas_call(
        _retention_kernel,
        grid=(bsz, RET_HEADS),
        in_specs=[
            pl.BlockSpec(memory_space=pltpu.SMEM),
            pl.BlockSpec((seq, qb), lambda b, h: (b, OFF_Q // qb + h)),
            pl.BlockSpec((seq, qb), lambda b, h: (b, OFF_K // qb + h)),
            pl.BlockSpec((seq, vb), lambda b, h: (b, OFF_V // vb + h)),
            pl.BlockSpec((seq, vb), lambda b, h: (b, OFF_G // vb + h)),
            pl.BlockSpec((seq, HALF), lambda b, h: (b, 0)),
            pl.BlockSpec((seq, HALF), lambda b, h: (b, 0)),
            pl.BlockSpec((None, RET_BLOCK, RET_BLOCK), lambda b, h: (h, 0, 0)),
            pl.BlockSpec((None, RET_BLOCK, HALF), lambda b, h: (h, 0, 0)),
            pl.BlockSpec((None, RET_BLOCK, HALF), lambda b, h: (h, 0, 0)),
        ],
        out_specs=pl.BlockSpec((seq, vb), lambda b, h: (b, h)),
        scratch_shapes=[pltpu.VMEM((seq, qb), BF16), pltpu.VMEM((seq, qb), BF16),
                        pltpu.VMEM((seq, qb), BF16), pltpu.VMEM((seq, qb), BF16)],
        out_shape=jax.ShapeDtypeStruct((n, RET_HEADS * vb), BF16),
        compiler_params=_params(("parallel", "parallel")),
        name="retention",
    )(gamma_chunk, proj, proj, proj, proj, cos, sin, decay_mask, xi_b, zeta_b)


def _mix_out_kernel(og_ref, cb_ref, cc_ref, ch_ref, ga_ref, gb_ref, h_ref, cw_ref,
                    wr_ref, wc_ref, wo_ref, o_ref, tail_ref, *, tiles_per_seq):
    t = og_ref.shape[0]

    @pl.when(pl.program_id(0) % tiles_per_seq == 0)
    def _():
        tail_ref[...] = jnp.zeros_like(tail_ref)

    yr = jnp.dot(og_ref[...], wr_ref[...], preferred_element_type=F32)
    row = lax.broadcasted_iota(jnp.int32, (t, CONV_LANES), 0)
    yc = None
    for c in range(D_MODEL // CONV_LANES):
        cs = slice(c * CONV_LANES, (c + 1) * CONV_LANES)
        u = cc_ref[:, cs].astype(F32) * ch_ref[:, cs].astype(F32)
        prev1 = tail_ref[SUBLANES - 1:SUBLANES, cs]
        prev2 = tail_ref[SUBLANES - 2:SUBLANES - 1, cs]
        u1 = jnp.where(row >= 1, pltpu.roll(u, 1, 0), prev1)
        u2 = jnp.where(row >= 2, pltpu.roll(u, 2, 0), jnp.where(row == 1, prev1, prev2))
        w = cw_ref[:, cs]
        y = w[0:1, :] * u2 + w[1:2, :] * u1 + w[2:3, :] * u
        cv = (cb_ref[:, cs].astype(F32) * y).astype(BF16)
        tail_ref[:, cs] = u[t - SUBLANES:t, :]
        part = jnp.dot(cv, wc_ref[cs, :], preferred_element_type=F32)
        yc = part if yc is None else yc + part
    ga = ga_ref[...].astype(F32)
    gb = gb_ref[...].astype(F32)
    merged = (_sigmoid(ga) * yr + _sigmoid(gb) * yc).astype(BF16)
    o_ref[...] = h_ref[...] + jnp.dot(merged, wo_ref[...], preferred_element_type=F32)


def _mix_out(og, proj, h, conv_w, w_ret_out, w_conv_out, w_out, layer, seq):
    n = h.shape[0]
    t = MIX_TILE
    rows = lambda i: (i, 0)
    pcol = lambda off: pl.BlockSpec((t, D_MODEL), lambda i: (i, off // D_MODEL))
    wspec = pl.BlockSpec((None, D_MODEL, D_MODEL), lambda i: (layer, 0, 0),
                         pipeline_mode=pl.Buffered(1))
    return pl.pallas_call(
        functools.partial(_mix_out_kernel, tiles_per_seq=seq // t),
        grid=(n // t,),
        in_specs=[pl.BlockSpec((t, D_MODEL), rows),
                  pcol(OFF_CB), pcol(OFF_CC), pcol(OFF_CH), pcol(OFF_GA), pcol(OFF_GB),
                  pl.BlockSpec((t, D_MODEL), rows),
                  pl.BlockSpec((None, CONV_K, D_MODEL), lambda i: (layer, 0, 0)),
                  wspec, wspec, wspec],
        out_specs=pl.BlockSpec((t, D_MODEL), rows),
        out_shape=jax.ShapeDtypeStruct((n, D_MODEL), F32),
        scratch_shapes=[pltpu.VMEM((SUBLANES, D_MODEL), F32)],
        compiler_params=_params(("arbitrary",)),
        name="mix_out",
    )(og, proj, proj, proj, proj, proj, h, conv_w, w_ret_out, w_conv_out, w_out)


def _normalise_tile(src_ref, g_ref, xb_ref, o_ref, residual):
    def body(r, c):
        sl = pl.ds(pl.multiple_of(r * SUB_ROWS, SUB_ROWS), SUB_ROWS)
        x = src_ref[sl, :]
        xb_ref[sl, :] = _rms_rows(x, g_ref[...]).astype(BF16)
        o_ref[sl, :] = x if residual else jnp.zeros_like(x)
        return c
    lax.fori_loop(0, ROW_TILE // SUB_ROWS, body, 0)


def _swiglu_rows(xb_ref, w1_ref, w3_ref, w2_ref, o_ref, rows):
    x = xb_ref[:rows, :]
    a = jnp.dot(x, w1_ref[...].astype(BF16), preferred_element_type=F32)
    b = jnp.dot(x, w3_ref[...].astype(BF16), preferred_element_type=F32)
    hh = (a * _sigmoid(a) * b).astype(BF16)
    o_ref[:rows, :] += jnp.dot(hh, w2_ref[...].astype(BF16), preferred_element_type=F32)


def _ffn_dense_kernel(x_ref, g_ref, w1_ref, w3_ref, w2_ref, o_ref, xb_ref):
    @pl.when(pl.program_id(1) == 0)
    def _():
        _normalise_tile(x_ref, g_ref, xb_ref, o_ref, True)

    _swiglu_rows(xb_ref, w1_ref, w3_ref, w2_ref, o_ref, ROW_TILE)


def _ffn_dense(x, g, g_row, w1, w3, w2, w_idx):
    rows = x.shape[0]
    tf = FFN_COLS
    nf = w1.shape[-1] // tf
    return pl.pallas_call(
        _ffn_dense_kernel,
        grid=(rows // ROW_TILE, nf),
        in_specs=[
            pl.BlockSpec((ROW_TILE, D_MODEL), lambda i, f: (i, 0), pipeline_mode=pl.Buffered(1)),
            pl.BlockSpec((None, 1, D_MODEL), lambda i, f: (g_row, 0, 0)),
            pl.BlockSpec((None, D_MODEL, tf), lambda i, f: (w_idx, 0, f)),
            pl.BlockSpec((None, D_MODEL, tf), lambda i, f: (w_idx, 0, f)),
            pl.BlockSpec((None, tf, D_MODEL), lambda i, f: (w_idx, f, 0)),
        ],
        out_specs=pl.BlockSpec((ROW_TILE, D_MODEL), lambda i, f: (i, 0)),
        scratch_shapes=[pltpu.VMEM((ROW_TILE, D_MODEL), BF16)],
        out_shape=jax.ShapeDtypeStruct((rows, D_MODEL), F32),
        compiler_params=_params(("parallel", "arbitrary"), FFN_VMEM_LIMIT),
        name="ffn_dense",
    )(x, g, w1, w3, w2)


def _experts_kernel(te_ref, tv_ref, src_ref, nsrc_ref, h_hbm, g_ref, w1_ref, w3_ref, w2_ref,
                    o_ref, land_ref, xb_ref, sem, *, rows_per_step):
    i = pl.program_id(0)
    f = pl.program_id(1)
    land_rows = land_ref.shape[0]

    def row_copy(idx_ref, r):
        tok = idx_ref[jnp.minimum(r, ROW_TILE - 1)]
        return pltpu.make_async_copy(h_hbm.at[pl.ds(tok, 1)], land_ref.at[pl.ds(r, 1)], sem)

    def wait_landing():
        pltpu.make_async_copy(h_hbm.at[pl.ds(0, land_rows)], land_ref, sem).wait()

    @pl.when(jnp.logical_and(i == 0, f == 0))
    def _():
        def body(r, c):
            row_copy(src_ref, r).start()
            return c
        lax.fori_loop(0, land_rows, body, 0)
        wait_landing()

    @pl.when(f == 0)
    def _():
        _normalise_tile(land_ref, g_ref, xb_ref, o_ref, False)

    def step(groups):
        base = f * rows_per_step
        for j in range(rows_per_step):
            row_copy(nsrc_ref, base + j).start()
        _swiglu_rows(xb_ref, w1_ref, w3_ref, w2_ref, o_ref, groups * SUB_ROWS)

    for groups in range(1, ROW_TILE // SUB_ROWS + 1):
        pl.when(tv_ref[i] == groups)(functools.partial(step, groups))

    @pl.when(jnp.logical_and(f == pl.num_programs(1) - 1, tv_ref[i] > 0))
    def _():
        wait_landing()


def _ffn_experts(h, g, g_row, w1, w3, w2, w_idx, tile_expert, tile_valid, src):
    n_tiles = tile_expert.shape[0]
    tf = FFN_COLS
    nf = w1.shape[-1] // tf
    rows_per_step = -(-ROW_TILE // nf)
    while (rows_per_step * nf) % SUBLANES:
        rows_per_step += 1
    land_rows = rows_per_step * nf

    def wcol(i, f, te, tv):
        return jnp.where(tv[i] > 0, f, nf - 1)

    return pl.pallas_call(
        functools.partial(_experts_kernel, rows_per_step=rows_per_step),
        grid_spec=pltpu.PrefetchScalarGridSpec(
            num_scalar_prefetch=2,
            grid=(n_tiles, nf),
            in_specs=[
                pl.BlockSpec((ROW_TILE,), lambda i, f, te, tv: (i,), memory_space=pltpu.SMEM),
                pl.BlockSpec((ROW_TILE,), lambda i, f, te, tv: (jnp.minimum(i + 1, n_tiles - 1),),
                             memory_space=pltpu.SMEM),
                pl.BlockSpec(memory_space=pl.ANY),
                pl.BlockSpec((None, 1, D_MODEL), lambda i, f, te, tv: (g_row, 0, 0)),
                pl.BlockSpec((None, None, D_MODEL, tf),
                             lambda i, f, te, tv: (w_idx, te[i], 0, wcol(i, f, te, tv))),
                pl.BlockSpec((None, None, D_MODEL, tf),
                             lambda i, f, te, tv: (w_idx, te[i], 0, wcol(i, f, te, tv))),
                pl.BlockSpec((None, None, tf, D_MODEL),
                             lambda i, f, te, tv: (w_idx, te[i], wcol(i, f, te, tv), 0)),
            ],
            out_specs=pl.BlockSpec((ROW_TILE, D_MODEL), lambda i, f, te, tv: (i, 0)),
            scratch_shapes=[pltpu.VMEM((land_rows, D_MODEL), F32),
                            pltpu.VMEM((ROW_TILE, D_MODEL), BF16),
                            pltpu.SemaphoreType.DMA(())],
        ),
        out_shape=jax.ShapeDtypeStruct((n_tiles * ROW_TILE, D_MODEL), F32),
        compiler_params=_params(("arbitrary", "arbitrary"), FFN_VMEM_LIMIT),
        name="ffn_experts",
    )(tile_expert, tile_valid, src, src, h, g, w1, w3, w2)


def _route_kernel(h_ref, g_ref, rhl_ref, info_ref, info_t_ref, cnt_ref, run_ref):
    t = h_ref.shape[0]

    @pl.when(pl.program_id(0) == 0)
    def _():
        run_ref[...] = jnp.zeros_like(run_ref)

    xn = _rms_rows(h_ref[...], g_ref[...])
    xh = xn.astype(BF16)
    xl = (xn - xh.astype(F32)).astype(BF16)
    hi = jnp.dot(xh, rhl_ref[...], preferred_element_type=F32)
    lo = jnp.dot(xl, rhl_ref[:, :LANES], preferred_element_type=F32)
    logits = hi[:, :LANES] + lo + hi[:, LANES:]
    lane = lax.broadcasted_iota(jnp.int32, (t, LANES), 1).astype(F32)
    neg = jnp.float32(-jnp.inf)
    logits = jnp.where(lane < N_EXPERTS, logits, neg)
    v1 = jnp.max(logits, axis=-1, keepdims=True)
    e1 = jnp.min(jnp.where(logits == v1, lane, float(LANES)), axis=-1, keepdims=True)
    rest = jnp.where(lane == e1, neg, logits)
    v2 = jnp.max(rest, axis=-1, keepdims=True)
    e2 = jnp.min(jnp.where(rest == v2, lane, float(LANES)), axis=-1, keepdims=True)
    ex = jnp.exp(v2 - v1)
    g1 = 1.0 / (1.0 + ex)
    g2 = ex / (1.0 + ex)

    onehot = jnp.where(jnp.logical_or(lane == e1, lane == e2), 1.0, 0.0)
    r_i = lax.broadcasted_iota(jnp.int32, (t, t), 0)
    c_i = lax.broadcasted_iota(jnp.int32, (t, t), 1)
    tri = jnp.where(c_i <= r_i, 1.0, 0.0).astype(BF16)
    cum = jnp.dot(tri, onehot.astype(BF16), preferred_element_type=F32)
    excl = cum - onehot + run_ref[...]
    rank1 = jnp.sum(jnp.where(lane == e1, excl, 0.0), axis=-1, keepdims=True)
    rank2 = jnp.sum(jnp.where(lane == e2, excl, 0.0), axis=-1, keepdims=True)
    run_ref[...] = run_ref[...] + cum[t - 1:t, :]

    info = jnp.where(lane == 0, e1, 0.0)
    info = jnp.where(lane == 1, e2, info)
    info = jnp.where(lane == 2, g1, info)
    info = jnp.where(lane == 3, g2, info)
    info = jnp.where(lane == 4, rank1, info)
    info = jnp.where(lane == 5, rank2, info)
    info_ref[...] = info
    info_t_ref[...] = info.T[:SUBLANES, :]
    cnt_ref[...] = jnp.broadcast_to(run_ref[...], cnt_ref.shape)


def _route(h, g, router, layer):
    n = h.shape[0]
    r = jnp.zeros((D_MODEL, LANES), F32).at[:, :N_EXPERTS].set(router)
    rh = r.astype(BF16)
    rl = (r - rh.astype(F32)).astype(BF16)
    rhl = jnp.concatenate([rh, rl], axis=1)
    t = ROUTE_TILE
    return pl.pallas_call(
        _route_kernel,
        grid=(n // t,),
        in_specs=[pl.BlockSpec((t, D_MODEL), lambda i: (i, 0)),
                  pl.BlockSpec((None, 1, D_MODEL), lambda i: (layer, 0, 0)),
                  pl.BlockSpec((D_MODEL, 2 * LANES), lambda i: (0, 0))],
        out_specs=[pl.BlockSpec((t, LANES), lambda i: (i, 0)),
                   pl.BlockSpec((SUBLANES, t), lambda i: (0, i)),
                   pl.BlockSpec((SUBLANES, LANES), lambda i: (0, 0))],
        out_shape=[jax.ShapeDtypeStruct((n, LANES), F32),
                   jax.ShapeDtypeStruct((SUBLANES, n), F32),
                   jax.ShapeDtypeStruct((SUBLANES, LANES), F32)],
        scratch_shapes=[pltpu.VMEM((1, LANES), F32)],
        compiler_params=_params(("arbitrary",)),
        name="route",
    )(h, g, rhl)


def _combine_kernel(pos_ref, npos_ref, h_ref, info_ref, g_ref, ys_ref, o_ref, buf_ref, sem, *,
                    final_norm):
    t = h_ref.shape[0]
    i = pl.program_id(0)
    cur = i % 2

    def gather(idx_ref, half):
        for r in range(t):
            for k in range(TOP_K):
                pltpu.make_async_copy(ys_ref.at[pl.ds(idx_ref[k * t + r], 1)],
                                      buf_ref.at[half, k, pl.ds(r, 1)], sem.at[half]).start()

    @pl.when(i == 0)
    def _():
        gather(pos_ref, 0)

    has_next = i + 1 < pl.num_programs(0)
    for half in range(2):
        pl.when(jnp.logical_and(has_next, cur != half))(functools.partial(gather, npos_ref, half))

    for k in range(TOP_K):
        pltpu.make_async_copy(ys_ref.at[pl.ds(0, t)], buf_ref.at[cur, k], sem.at[cur]).wait()

    info = info_ref[...]
    lane = lax.broadcasted_iota(jnp.int32, info.shape, 1)
    g1 = jnp.sum(jnp.where(lane == 2, info, 0.0), axis=-1, keepdims=True)
    g2 = jnp.sum(jnp.where(lane == 3, info, 0.0), axis=-1, keepdims=True)
    y = h_ref[...] + g1 * buf_ref[cur, 0] + g2 * buf_ref[cur, 1]
    o_ref[...] = _rms_rows(y, g_ref[...]) if final_norm else y


def _combine(h, info, norm_final, ys, pos_tiles, final_norm):
    n = h.shape[0]
    t = COMBINE_TILE
    last = n // t - 1
    return pl.pallas_call(
        functools.partial(_combine_kernel, final_norm=final_norm),
        grid=(n // t,),
        in_specs=[pl.BlockSpec((TOP_K * t,), lambda i: (i,), memory_space=pltpu.SMEM),
                  pl.BlockSpec((TOP_K * t,), lambda i: (jnp.minimum(i + 1, last),),
                               memory_space=pltpu.SMEM),
                  pl.BlockSpec((t, D_MODEL), lambda i: (i, 0)),
                  pl.BlockSpec((t, LANES), lambda i: (i, 0)),
                  pl.BlockSpec((1, D_MODEL), lambda i: (0, 0)),
                  pl.BlockSpec(memory_space=pl.ANY)],
        out_specs=pl.BlockSpec((t, D_MODEL), lambda i: (i, 0)),
        out_shape=jax.ShapeDtypeStruct((n, D_MODEL), F32),
        scratch_shapes=[pltpu.VMEM((2, TOP_K, t, D_MODEL), F32), pltpu.SemaphoreType.DMA((2,))],
        compiler_params=_params(("arbitrary",)),
        name="combine",
    )(pos_tiles, pos_tiles, h, info, norm_final.reshape(1, D_MODEL), ys)


def _row_tokens_kernel(pos_ref, pad_ref, src_ref):
    i = pl.program_id(0)
    t = pos_ref.shape[0] // TOP_K
    n_ranges = pad_ref.shape[0] // 2

    @pl.when(i == 0)
    def _():
        for e in range(n_ranges):
            def zero(r, c):
                src_ref[r] = jnp.int32(0)
                return c
            lax.fori_loop(pad_ref[e], pad_ref[n_ranges + e], zero, 0)

    def body(r, c):
        for k in range(TOP_K):
            src_ref[pos_ref[k * t + r]] = i * t + r
        return c
    lax.fori_loop(0, t, body, 0, unroll=8)


def _row_tokens(pos_tiles, pad_ranges, rows):
    t = INVERT_TILE
    return pl.pallas_call(
        _row_tokens_kernel,
        grid=(pos_tiles.shape[0] // (TOP_K * t),),
        in_specs=[pl.BlockSpec((TOP_K * t,), lambda i: (i,), memory_space=pltpu.SMEM),
                  pl.BlockSpec(memory_space=pltpu.SMEM)],
        out_specs=pl.BlockSpec(memory_space=pltpu.SMEM),
        out_shape=jax.ShapeDtypeStruct((rows,), jnp.int32),
        compiler_params=_params(("arbitrary",)),
        name="row_tokens",
    )(pos_tiles, pad_ranges)


def _pos_tiles(pos1, pos2, tile):
    nt = pos1.shape[0] // tile
    return jnp.stack([pos1.reshape(nt, tile), pos2.reshape(nt, tile)], axis=1).reshape(-1)


def _moe_layer(h, norm_g, layer, router, w1, w3, w2, moe_idx, norm_final, final_norm):
    n = h.shape[0]
    info, info_t, cnt = _route(h, norm_g, router, layer)
    counts = cnt[0, :N_EXPERTS].astype(jnp.int32)
    padded = ((counts + ROW_TILE - 1) // ROW_TILE) * ROW_TILE
    ends = jnp.cumsum(padded)
    starts = ends - padded
    e1 = info_t[0].astype(jnp.int32)
    e2 = info_t[1].astype(jnp.int32)
    pos1 = starts[e1] + info_t[4].astype(jnp.int32)
    pos2 = starts[e2] + info_t[5].astype(jnp.int32)

    n_tiles = (n * TOP_K) // ROW_TILE + N_EXPERTS
    tile_start = jnp.arange(n_tiles, dtype=jnp.int32) * ROW_TILE
    in_use = tile_start < ends[-1]
    tile_expert = jnp.sum((tile_start[:, None] >= ends[None, :]).astype(jnp.int32), axis=1)
    last_expert = jnp.sum((ends[-1] - 1 >= ends).astype(jnp.int32))
    tile_expert = jnp.where(in_use, tile_expert, last_expert).astype(jnp.int32)
    tile_rows = jnp.clip((starts + counts)[tile_expert] - tile_start, 0, ROW_TILE)
    tile_valid = jnp.where(in_use, (tile_rows + SUB_ROWS - 1) // SUB_ROWS, 0).astype(jnp.int32)

    n_rows = n_tiles * ROW_TILE
    pad_ranges = jnp.concatenate([starts + counts, ends[-1:], ends,
                                  jnp.full((1,), n_rows, jnp.int32)]).astype(jnp.int32)
    src = _row_tokens(_pos_tiles(pos1, pos2, INVERT_TILE), pad_ranges, n_rows)

    ys = _ffn_experts(h, norm_g, layer, w1, w3, w2, moe_idx, tile_expert, tile_valid, src)
    return _combine(h, info, norm_final, ys, _pos_tiles(pos1, pos2, COMBINE_TILE), final_norm)


def kernel(x, positions, norm_mix, norm_ffn, norm_final, w_in, conv_w, w_ret_out, w_conv_out,
           w_out, ffn_w1, ffn_w3, ffn_w2, router, moe_w1, moe_w3, moe_w2):
    bsz, seq, _ = x.shape
    depth = w_in.shape[0]
    n = bsz * seq
    h = x.reshape(n, D_MODEL)
    norm_mix = norm_mix.reshape(depth, 1, D_MODEL)
    norm_ffn = norm_ffn.reshape(depth, 1, D_MODEL)
    cos, sin = _rope_tables(positions)
    consts = _retention_consts()
    w_ret_out, w_conv_out, w_out = (w.astype(BF16) for w in (w_ret_out, w_conv_out, w_out))
    for layer in range(depth):
        proj = _inproj(_norm_rows(h, norm_mix, layer, BF16), w_in, layer)
        og = _retention(proj, cos, sin, consts, bsz, seq)
        h = _mix_out(og, proj, h, conv_w, w_ret_out, w_conv_out, w_out, layer, seq)
        i = layer // 2
        last = layer == depth - 1
        if layer % 2 == 0:
            h = _ffn_dense(h, norm_ffn, layer, ffn_w1, ffn_w3, ffn_w2, i)
            if last:
                h = _norm_rows(h, norm_final.reshape(1, 1, D_MODEL), 0, F32)
        else:
            h = _moe_layer(h, norm_ffn, layer, router[i], moe_w1, moe_w3, moe_w2, i,
                           norm_final, last)
    return h.reshape(bsz, seq, D_MODEL)
```

```python
import functools

import jax
import jax.numpy as jnp
from jax import lax
from jax.experimental import pallas as pl
from jax.experimental.pallas import tpu as pltpu

F32 = jnp.float32
BF16 = jnp.bfloat16

D_MODEL = 2048
RET_HEADS = 4
RET_QK_DIM = 256
RET_V_DIM = 512
RET_BLOCK = 256
ROPE_BASE = 10000.0
CONV_K = 3
N_EXPERTS = 8
TOP_K = 2
EPS = 1e-6
HALF = RET_QK_DIM // 2

OFF_Q, OFF_K, OFF_V, OFF_G = 0, 1024, 2048, 4096
OFF_CB, OFF_CC, OFF_CH, OFF_GA, OFF_GB = 6144, 8192, 10240, 12288, 14336
D_IN_TOTAL = 16384

LANES = 128
SUBLANES = 8
VMEM_LIMIT = 56 * 1024 * 1024

ROW_TILE = 1024
ROPE_ROWS = 2048
NORM_ROWS = 512
INPROJ_ROWS = 2048
INPROJ_COLS = 1024
SUB_ROWS = 256
MIX_TILE = 256
FFN_COLS = 512
FFN_VMEM_LIMIT = 61 * 1024 * 1024
ROUTE_TILE = 512
CONV_LANES = 512
COMBINE_TILE = 256
INVERT_TILE = 2048


def _params(sem, vmem=VMEM_LIMIT):
    return pltpu.CompilerParams(dimension_semantics=sem, vmem_limit_bytes=vmem)


def _rms_rows(x, g):
    ms = jnp.mean(x * x, axis=-1, keepdims=True)
    return x * lax.rsqrt(ms + EPS) * g


def _sigmoid(x):
    return 1.0 / (1.0 + jnp.exp(-x))


def _rope_kernel(pos_ref, invf_ref, cos_ref, sin_ref):
    ang = pos_ref[...].astype(F32) * invf_ref[...]
    cos_ref[...] = jnp.cos(ang)
    sin_ref[...] = jnp.sin(ang)


def _rope_tables(positions):
    n = positions.size
    inv_freq = ROPE_BASE ** (-jnp.arange(HALF, dtype=F32) / HALF)
    rows = ROPE_ROWS
    return pl.pallas_call(
        _rope_kernel,
        grid=(n // rows,),
        in_specs=[pl.BlockSpec((rows, 1), lambda i: (i, 0)),
                  pl.BlockSpec((1, HALF), lambda i: (0, 0))],
        out_specs=[pl.BlockSpec((rows, HALF), lambda i: (i, 0)),
                   pl.BlockSpec((rows, HALF), lambda i: (i, 0))],
        out_shape=[jax.ShapeDtypeStruct((n, HALF), F32)] * 2,
        compiler_params=_params(("parallel",)),
        name="rope_tables",
    )(positions.reshape(n, 1), inv_freq.reshape(1, HALF))


def _norm_kernel(h_ref, g_ref, o_ref):
    o_ref[...] = _rms_rows(h_ref[...], g_ref[...]).astype(o_ref.dtype)


def _norm_rows(h, g, row, dtype):
    n = h.shape[0]
    t = NORM_ROWS
    return pl.pallas_call(
        _norm_kernel,
        grid=(n // t,),
        in_specs=[pl.BlockSpec((t, D_MODEL), lambda i: (i, 0)),
                  pl.BlockSpec((None, 1, D_MODEL), lambda i: (row, 0, 0))],
        out_specs=pl.BlockSpec((t, D_MODEL), lambda i: (i, 0)),
        out_shape=jax.ShapeDtypeStruct((n, D_MODEL), dtype),
        compiler_params=_params(("parallel",)),
        name="norm_rows",
    )(h, g)


def _inproj_kernel(x_ref, w_ref, o_ref, wb_ref):
    @pl.when(pl.program_id(1) == 0)
    def _():
        wb_ref[...] = w_ref[...].astype(BF16)

    for r in range(x_ref.shape[0] // ROW_TILE):
        rs = slice(r * ROW_TILE, (r + 1) * ROW_TILE)
        o_ref[rs, :] = jnp.dot(x_ref[rs, :], wb_ref[...],
                               preferred_element_type=F32).astype(o_ref.dtype)


def _inproj(xn, w_in, layer):
    n = xn.shape[0]
    tn = INPROJ_COLS
    tm = INPROJ_ROWS
    return pl.pallas_call(
        _inproj_kernel,
        grid=(D_IN_TOTAL // tn, n // tm),
        in_specs=[pl.BlockSpec((tm, D_MODEL), lambda j, i: (i, 0)),
                  pl.BlockSpec((None, D_MODEL, tn), lambda j, i: (layer, 0, j))],
        out_specs=pl.BlockSpec((tm, tn), lambda j, i: (i, j)),
        out_shape=jax.ShapeDtypeStruct((n, D_IN_TOTAL), BF16),
        scratch_shapes=[pltpu.VMEM((D_MODEL, tn), BF16)],
        compiler_params=_params(("parallel", "arbitrary")),
        name="inproj",
    )(xn, w_in)


def _retention_kernel(gc_ref, q_ref, k_ref, v_ref, g_ref, cos_ref, sin_ref, dm_ref, xi_ref,
                      zeta_ref, o_ref, qr_ref, qx_ref, kr_ref, kz_ref):
    seq = q_ref.shape[0]
    scale = RET_QK_DIM ** -0.5

    def rot(r, c):
        sl = pl.ds(pl.multiple_of(r * RET_BLOCK, RET_BLOCK), RET_BLOCK)
        cos = cos_ref[sl, :]
        sin = sin_ref[sl, :]
        xi = xi_ref[...]
        zeta = zeta_ref[...]
        q = q_ref[sl, :].astype(F32)
        q1, q2 = q[:, :HALF], q[:, HALF:]
        qa = (q1 * cos - q2 * sin) * scale
        qb = (q1 * sin + q2 * cos) * scale
        qr_ref[sl, :HALF] = qa.astype(BF16)
        qr_ref[sl, HALF:] = qb.astype(BF16)
        qx_ref[sl, :HALF] = (qa * xi).astype(BF16)
        qx_ref[sl, HALF:] = (qb * xi).astype(BF16)
        k = k_ref[sl, :].astype(F32)
        k1, k2 = k[:, :HALF], k[:, HALF:]
        ka = k1 * cos - k2 * sin
        kb = k1 * sin + k2 * cos
        kr_ref[sl, :HALF] = ka.astype(BF16)
        kr_ref[sl, HALF:] = kb.astype(BF16)
        kz_ref[sl, :HALF] = (ka * zeta).astype(BF16)
        kz_ref[sl, HALF:] = (kb * zeta).astype(BF16)
        return c
    lax.fori_loop(0, seq // RET_BLOCK, rot, 0)

    gamma_block = gc_ref[pl.program_id(1)]
    n_blocks = seq // RET_BLOCK
    state = None
    for c in range(n_blocks):
        sl = slice(c * RET_BLOCK, (c + 1) * RET_BLOCK)
        vc = v_ref[sl, :]
        scores = lax.dot_general(qr_ref[sl, :], kr_ref[sl, :], (((1,), (1,)), ((), ())),
                                 preferred_element_type=F32) * dm_ref[...]
        o = jnp.dot(scores.astype(BF16), vc, preferred_element_type=F32)
        if state is not None:
            o = o + jnp.dot(qx_ref[sl, :], state.astype(BF16), preferred_element_type=F32)
        if c + 1 < n_blocks:
            new = lax.dot_general(kz_ref[sl, :], vc, (((0,), (0,)), ((), ())),
                                  preferred_element_type=F32)
            state = new if state is None else state * gamma_block + new
        mu = jnp.mean(o, axis=-1, keepdims=True)
        d = o - mu
        var = jnp.mean(d * d, axis=-1, keepdims=True)
        on = d * lax.rsqrt(var + EPS)
        g = g_ref[sl, :].astype(F32)
        o_ref[sl, :] = (on * (g * _sigmoid(g))).astype(o_ref.dtype)


def _retention_consts():
    h = RET_HEADS
    log_gamma = jnp.log1p(-jnp.power(2.0, -5.0 - jnp.arange(h, dtype=F32)))
    idx = jnp.arange(RET_BLOCK, dtype=F32)
    diff = idx[:, None] - idx[None, :]
    causal = diff >= 0
    decay_mask = jnp.where(causal[None],
                           jnp.exp(jnp.where(causal, diff, 0.0)[None] * log_gamma[:, None, None]),
                           0.0)
    xi = jnp.exp((idx + 1.0)[None] * log_gamma[:, None])
    zeta = jnp.exp((RET_BLOCK - 1.0 - idx)[None] * log_gamma[:, None])
    gamma_block = jnp.exp(RET_BLOCK * log_gamma)
    xi_b = jnp.broadcast_to(xi[:, :, None], (h, RET_BLOCK, HALF))
    zeta_b = jnp.broadcast_to(zeta[:, :, None], (h, RET_BLOCK, HALF))
    return decay_mask, xi_b, zeta_b, gamma_block


def _retention(proj, cos, sin, consts, bsz, seq):
    decay_mask, xi_b, zeta_b, gamma_chunk = consts
    n = proj.shape[0]
    qb, vb = RET_QK_DIM, RET_V_DIM
    return pl.pallas_call(
        _retention_kernel,
        grid=(bsz, RET_HEADS),
        in_specs=[
            pl.BlockSpec(memory_space=pltpu.SMEM),
            pl.BlockSpec((seq, qb), lambda b, h: (b, OFF_Q // qb + h)),
            pl.BlockSpec((seq, qb), lambda b, h: (b, OFF_K // qb + h)),
            pl.BlockSpec((seq, vb), lambda b, h: (b, OFF_V // vb + h)),
            pl.BlockSpec((seq, vb), lambda b, h: (b, OFF_G // vb + h)),
            pl.BlockSpec((seq, HALF), lambda b, h: (b, 0)),
            pl.BlockSpec((seq, HALF), lambda b, h: (b, 0)),
            pl.BlockSpec((None, RET_BLOCK, RET_BLOCK), lambda b, h: (h, 0, 0)),
            pl.BlockSpec((None, RET_BLOCK, HALF), lambda b, h: (h, 0, 0)),
            pl.BlockSpec((None, RET_BLOCK, HALF), lambda b, h: (h, 0, 0)),
        ],
        out_specs=pl.BlockSpec((seq, vb), lambda b, h: (b, h)),
        scratch_shapes=[pltpu.VMEM((seq, qb), BF16), pltpu.VMEM((seq, qb), BF16),
                        pltpu.VMEM((seq, qb), BF16), pltpu.VMEM((seq, qb), BF16)],
        out_shape=jax.ShapeDtypeStruct((n, RET_HEADS * vb), BF16),
        compiler_params=_params(("parallel", "parallel")),
        name="retention",
    )(gamma_chunk, proj, proj, proj, proj, cos, sin, decay_mask, xi_b, zeta_b)


def _mix_out_kernel(og_ref, cb_ref, cc_ref, ch_ref, ga_ref, gb_ref, h_ref, cw_ref,
                    wr_ref, wc_ref, wo_ref, o_ref, tail_ref, *, tiles_per_seq):
    t = og_ref.shape[0]

    @pl.when(pl.program_id(0) % tiles_per_seq == 0)
    def _():
        tail_ref[...] = jnp.zeros_like(tail_ref)

    yr = jnp.dot(og_ref[...], wr_ref[...], preferred_element_type=F32)
    row = lax.broadcasted_iota(jnp.int32, (t, CONV_LANES), 0)
    yc = None
    for c in range(D_MODEL // CONV_LANES):
        cs = slice(c * CONV_LANES, (c + 1) * CONV_LANES)
        u = cc_ref[:, cs].astype(F32) * ch_ref[:, cs].astype(F32)
        prev1 = tail_ref[SUBLANES - 1:SUBLANES, cs]
        prev2 = tail_ref[SUBLANES - 2:SUBLANES - 1, cs]
        u1 = jnp.where(row >= 1, pltpu.roll(u, 1, 0), prev1)
        u2 = jnp.where(row >= 2, pltpu.roll(u, 2, 0), jnp.where(row == 1, prev1, prev2))
        w = cw_ref[:, cs]
        y = w[0:1, :] * u2 + w[1:2, :] * u1 + w[2:3, :] * u
        cv = (cb_ref[:, cs].astype(F32) * y).astype(BF16)
        tail_ref[:, cs] = u[t - SUBLANES:t, :]
        part = jnp.dot(cv, wc_ref[cs, :], preferred_element_type=F32)
        yc = part if yc is None else yc + part
    ga = ga_ref[...].astype(F32)
    gb = gb_ref[...].astype(F32)
    merged = (_sigmoid(ga) * yr + _sigmoid(gb) * yc).astype(BF16)
    o_ref[...] = h_ref[...] + jnp.dot(merged, wo_ref[...], preferred_element_type=F32)


def _mix_out(og, proj, h, conv_w, w_ret_out, w_conv_out, w_out, layer, seq):
    n = h.shape[0]
    t = MIX_TILE
    rows = lambda i: (i, 0)
    pcol = lambda off: pl.BlockSpec((t, D_MODEL), lambda i: (i, off // D_MODEL))
    wspec = pl.BlockSpec((None, D_MODEL, D_MODEL), lambda i: (layer, 0, 0),
                         pipeline_mode=pl.Buffered(1))
    return pl.pallas_call(
        functools.partial(_mix_out_kernel, tiles_per_seq=seq // t),
        grid=(n // t,),
        in_specs=[pl.BlockSpec((t, D_MODEL), rows),
                  pcol(OFF_CB), pcol(OFF_CC), pcol(OFF_CH), pcol(OFF_GA), pcol(OFF_GB),
                  pl.BlockSpec((t, D_MODEL), rows),
                  pl.BlockSpec((None, CONV_K, D_MODEL), lambda i: (layer, 0, 0)),
                  wspec, wspec, wspec],
        out_specs=pl.BlockSpec((t, D_MODEL), rows),
        out_shape=jax.ShapeDtypeStruct((n, D_MODEL), F32),
        scratch_shapes=[pltpu.VMEM((SUBLANES, D_MODEL), F32)],
        compiler_params=_params(("arbitrary",)),
        name="mix_out",
    )(og, proj, proj, proj, proj, proj, h, conv_w, w_ret_out, w_conv_out, w_out)


def _normalise_tile(src_ref, g_ref, xb_ref, o_ref, residual):
    def body(r, c):
        sl = pl.ds(pl.multiple_of(r * SUB_ROWS, SUB_ROWS), SUB_ROWS)
        x = src_ref[sl, :]
        xb_ref[sl, :] = _rms_rows(x, g_ref[...]).astype(BF16)
        o_ref[sl, :] = x if residual else jnp.zeros_like(x)
        return c
    lax.fori_loop(0, ROW_TILE // SUB_ROWS, body, 0)


def _swiglu_rows(xb_ref, w1_ref, w3_ref, w2_ref, o_ref, rows):
    x = xb_ref[:rows, :]
    a = jnp.dot(x, w1_ref[...].astype(BF16), preferred_element_type=F32)
    b = jnp.dot(x, w3_ref[...].astype(BF16), preferred_element_type=F32)
    hh = (a * _sigmoid(a) * b).astype(BF16)
    o_ref[:rows, :] += jnp.dot(hh, w2_ref[...].astype(BF16), preferred_element_type=F32)


def _ffn_dense_kernel(x_ref, g_ref, w1_ref, w3_ref, w2_ref, o_ref, xb_ref):
    @pl.when(pl.program_id(1) == 0)
    def _():
        _normalise_tile(x_ref, g_ref, xb_ref, o_ref, True)

    _swiglu_rows(xb_ref, w1_ref, w3_ref, w2_ref, o_ref, ROW_TILE)


def _ffn_dense(x, g, g_row, w1, w3, w2, w_idx):
    rows = x.shape[0]
    tf = FFN_COLS
    nf = w1.shape[-1] // tf
    return pl.pallas_call(
        _ffn_dense_kernel,
        grid=(rows // ROW_TILE, nf),
        in_specs=[
            pl.BlockSpec((ROW_TILE, D_MODEL), lambda i, f: (i, 0), pipeline_mode=pl.Buffered(1)),
            pl.BlockSpec((None, 1, D_MODEL), lambda i, f: (g_row, 0, 0)),
            pl.BlockSpec((None, D_MODEL, tf), lambda i, f: (w_idx, 0, f)),
            pl.BlockSpec((None, D_MODEL, tf), lambda i, f: (w_idx, 0, f)),
            pl.BlockSpec((None, tf, D_MODEL), lambda i, f: (w_idx, f, 0)),
        ],
        out_specs=pl.BlockSpec((ROW_TILE, D_MODEL), lambda i, f: (i, 0)),
        scratch_shapes=[pltpu.VMEM((ROW_TILE, D_MODEL), BF16)],
        out_shape=jax.ShapeDtypeStruct((rows, D_MODEL), F32),
        compiler_params=_params(("parallel", "arbitrary"), FFN_VMEM_LIMIT),
        name="ffn_dense",
    )(x, g, w1, w3, w2)


def _experts_kernel(te_ref, tv_ref, src_ref, nsrc_ref, h_hbm, g_ref, w1_ref, w3_ref, w2_ref,
                    o_ref, land_ref, xb_ref, sem, *, rows_per_step):
    i = pl.program_id(0)
    f = pl.program_id(1)
    land_rows = land_ref.shape[0]

    def row_copy(idx_ref, r):
        tok = idx_ref[jnp.minimum(r, ROW_TILE - 1)]
        return pltpu.make_async_copy(h_hbm.at[pl.ds(tok, 1)], land_ref.at[pl.ds(r, 1)], sem)

    def wait_landing():
        pltpu.make_async_copy(h_hbm.at[pl.ds(0, land_rows)], land_ref, sem).wait()

    @pl.when(jnp.logical_and(i == 0, f == 0))
    def _():
        def body(r, c):
            row_copy(src_ref, r).start()
            return c
        lax.fori_loop(0, land_rows, body, 0)
        wait_landing()

    @pl.when(f == 0)
    def _():
        _normalise_tile(land_ref, g_ref, xb_ref, o_ref, False)

    def step(groups):
        base = f * rows_per_step
        for j in range(rows_per_step):
            row_copy(nsrc_ref, base + j).start()
        _swiglu_rows(xb_ref, w1_ref, w3_ref, w2_ref, o_ref, groups * SUB_ROWS)

    for groups in range(1, ROW_TILE // SUB_ROWS + 1):
        pl.when(tv_ref[i] == groups)(functools.partial(step, groups))

    @pl.when(jnp.logical_and(f == pl.num_programs(1) - 1, tv_ref[i] > 0))
    def _():
        wait_landing()


def _ffn_experts(h, g, g_row, w1, w3, w2, w_idx, tile_expert, tile_valid, src):
    n_tiles = tile_expert.shape[0]
    tf = FFN_COLS
    nf = w1.shape[-1] // tf
    rows_per_step = -(-ROW_TILE // nf)
    while (rows_per_step * nf) % SUBLANES:
        rows_per_step += 1
    land_rows = rows_per_step * nf

    def wcol(i, f, te, tv):
        return jnp.where(tv[i] > 0, f, nf - 1)

    return pl.pallas_call(
        functools.partial(_experts_kernel, rows_per_step=rows_per_step),
        grid_spec=pltpu.PrefetchScalarGridSpec(
            num_scalar_prefetch=2,
            grid=(n_tiles, nf),
            in_specs=[
                pl.BlockSpec((ROW_TILE,), lambda i, f, te, tv: (i,), memory_space=pltpu.SMEM),
                pl.BlockSpec((ROW_TILE,), lambda i, f, te, tv: (jnp.minimum(i + 1, n_tiles - 1),),
                             memory_space=pltpu.SMEM),
                pl.BlockSpec(memory_space=pl.ANY),
                pl.BlockSpec((None, 1, D_MODEL), lambda i, f, te, tv: (g_row, 0, 0)),
                pl.BlockSpec((None, None, D_MODEL, tf),
                             lambda i, f, te, tv: (w_idx, te[i], 0, wcol(i, f, te, tv))),
                pl.BlockSpec((None, None, D_MODEL, tf),
                             lambda i, f, te, tv: (w_idx, te[i], 0, wcol(i, f, te, tv))),
                pl.BlockSpec((None, None, tf, D_MODEL),
                             lambda i, f, te, tv: (w_idx, te[i], wcol(i, f, te, tv), 0)),
            ],
            out_specs=pl.BlockSpec((ROW_TILE, D_MODEL), lambda i, f, te, tv: (i, 0)),
            scratch_shapes=[pltpu.VMEM((land_rows, D_MODEL), F32),
                            pltpu.VMEM((ROW_TILE, D_MODEL), BF16),
                            pltpu.SemaphoreType.DMA(())],
        ),
        out_shape=jax.ShapeDtypeStruct((n_tiles * ROW_TILE, D_MODEL), F32),
        compiler_params=_params(("arbitrary", "arbitrary"), FFN_VMEM_LIMIT),
        name="ffn_experts",
    )(tile_expert, tile_valid, src, src, h, g, w1, w3, w2)


def _route_kernel(h_ref, g_ref, rhl_ref, info_ref, info_t_ref, cnt_ref, run_ref):
    t = h_ref.shape[0]

    @pl.when(pl.program_id(0) == 0)
    def _():
        run_ref[...] = jnp.zeros_like(run_ref)

    xn = _rms_rows(h_ref[...], g_ref[...])
    xh = xn.astype(BF16)
    xl = (xn - xh.astype(F32)).astype(BF16)
    hi = jnp.dot(xh, rhl_ref[...], preferred_element_type=F32)
    lo = jnp.dot(xl, rhl_ref[:, :LANES], preferred_element_type=F32)
    logits = hi[:, :LANES] + lo + hi[:, LANES:]
    lane = lax.broadcasted_iota(jnp.int32, (t, LANES), 1).astype(F32)
    neg = jnp.float32(-jnp.inf)
    logits = jnp.where(lane < N_EXPERTS, logits, neg)
    v1 = jnp.max(logits, axis=-1, keepdims=True)
    e1 = jnp.min(jnp.where(logits == v1, lane, float(LANES)), axis=-1, keepdims=True)
    rest = jnp.where(lane == e1, neg, logits)
    v2 = jnp.max(rest, axis=-1, keepdims=True)
    e2 = jnp.min(jnp.where(rest == v2, lane, float(LANES)), axis=-1, keepdims=True)
    ex = jnp.exp(v2 - v1)
    g1 = 1.0 / (1.0 + ex)
    g2 = ex / (1.0 + ex)

    onehot = jnp.where(jnp.logical_or(lane == e1, lane == e2), 1.0, 0.0)
    r_i = lax.broadcasted_iota(jnp.int32, (t, t), 0)
    c_i = lax.broadcasted_iota(jnp.int32, (t, t), 1)
    tri = jnp.where(c_i <= r_i, 1.0, 0.0).astype(BF16)
    cum = jnp.dot(tri, onehot.astype(BF16), preferred_element_type=F32)
    excl = cum - onehot + run_ref[...]
    rank1 = jnp.sum(jnp.where(lane == e1, excl, 0.0), axis=-1, keepdims=True)
    rank2 = jnp.sum(jnp.where(lane == e2, excl, 0.0), axis=-1, keepdims=True)
    run_ref[...] = run_ref[...] + cum[t - 1:t, :]

    info = jnp.where(lane == 0, e1, 0.0)
    info = jnp.where(lane == 1, e2, info)
    info = jnp.where(lane == 2, g1, info)
    info = jnp.where(lane == 3, g2, info)
    info = jnp.where(lane == 4, rank1, info)
    info = jnp.where(lane == 5, rank2, info)
    info_ref[...] = info
    info_t_ref[...] = info.T[:SUBLANES, :]
    cnt_ref[...] = jnp.broadcast_to(run_ref[...], cnt_ref.shape)


def _route(h, g, router, layer):
    n = h.shape[0]
    r = jnp.zeros((D_MODEL, LANES), F32).at[:, :N_EXPERTS].set(router)
    rh = r.astype(BF16)
    rl = (r - rh.astype(F32)).astype(BF16)
    rhl = jnp.concatenate([rh, rl], axis=1)
    t = ROUTE_TILE
    return pl.pallas_call(
        _route_kernel,
        grid=(n // t,),
        in_specs=[pl.BlockSpec((t, D_MODEL), lambda i: (i, 0)),
                  pl.BlockSpec((None, 1, D_MODEL), lambda i: (layer, 0, 0)),
                  pl.BlockSpec((D_MODEL, 2 * LANES), lambda i: (0, 0))],
        out_specs=[pl.BlockSpec((t, LANES), lambda i: (i, 0)),
                   pl.BlockSpec((SUBLANES, t), lambda i: (0, i)),
                   pl.BlockSpec((SUBLANES, LANES), lambda i: (0, 0))],
        out_shape=[jax.ShapeDtypeStruct((n, LANES), F32),
                   jax.ShapeDtypeStruct((SUBLANES, n), F32),
                   jax.ShapeDtypeStruct((SUBLANES, LANES), F32)],
        scratch_shapes=[pltpu.VMEM((1, LANES), F32)],
        compiler_params=_params(("arbitrary",)),
        name="route",
    )(h, g, rhl)


def _combine_kernel(pos_ref, npos_ref, h_ref, info_ref, g_ref, ys_ref, o_ref, buf_ref, sem, *,
                    final_norm):
    t = h_ref.shape[0]
    i = pl.program_id(0)
    cur = i % 2

    def gather(idx_ref, half):
        for r in range(t):
            for k in range(TOP_K):
                pltpu.make_async_copy(ys_ref.at[pl.ds(idx_ref[k * t + r], 1)],
                                      buf_ref.at[half, k, pl.ds(r, 1)],
                                      sem.at[half]).start(priority=(r * TOP_K + k) % 2)

    @pl.when(i == 0)
    def _():
        gather(pos_ref, 0)

    has_next = i + 1 < pl.num_programs(0)
    for half in range(2):
        pl.when(jnp.logical_and(has_next, cur != half))(functools.partial(gather, npos_ref, half))

    for k in range(TOP_K):
        pltpu.make_async_copy(ys_ref.at[pl.ds(0, t)], buf_ref.at[cur, k], sem.at[cur]).wait()

    info = info_ref[...]
    lane = lax.broadcasted_iota(jnp.int32, info.shape, 1)
    g1 = jnp.sum(jnp.where(lane == 2, info, 0.0), axis=-1, keepdims=True)
    g2 = jnp.sum(jnp.where(lane == 3, info, 0.0), axis=-1, keepdims=True)
    y = h_ref[...] + g1 * buf_ref[cur, 0] + g2 * buf_ref[cur, 1]
    o_ref[...] = _rms_rows(y, g_ref[...]) if final_norm else y


def _combine(h, info, norm_final, ys, pos_tiles, final_norm):
    n = h.shape[0]
    t = COMBINE_TILE
    last = n // t - 1
    return pl.pallas_call(
        functools.partial(_combine_kernel, final_norm=final_norm),
        grid=(n // t,),
        in_specs=[pl.BlockSpec((TOP_K * t,), lambda i: (i,), memory_space=pltpu.SMEM),
                  pl.BlockSpec((TOP_K * t,), lambda i: (jnp.minimum(i + 1, last),),
                               memory_space=pltpu.SMEM),
                  pl.BlockSpec((t, D_MODEL), lambda i: (i, 0)),
                  pl.BlockSpec((t, LANES), lambda i: (i, 0)),
                  pl.BlockSpec((1, D_MODEL), lambda i: (0, 0)),
                  pl.BlockSpec(memory_space=pl.ANY)],
        out_specs=pl.BlockSpec((t, D_MODEL), lambda i: (i, 0)),
        out_shape=jax.ShapeDtypeStruct((n, D_MODEL), F32),
        scratch_shapes=[pltpu.VMEM((2, TOP_K, t, D_MODEL), F32), pltpu.SemaphoreType.DMA((2,))],
        compiler_params=_params(("arbitrary",)),
        name="combine",
    )(pos_tiles, pos_tiles, h, info, norm_final.reshape(1, D_MODEL), ys)


def _row_tokens_kernel(pos_ref, pad_ref, src_ref):
    i = pl.program_id(0)
    t = pos_ref.shape[0] // TOP_K
    n_ranges = pad_ref.shape[0] // 2

    @pl.when(i == 0)
    def _():
        for e in range(n_ranges):
            def zero(r, c):
                src_ref[r] = jnp.int32(0)
                return c
            lax.fori_loop(pad_ref[e], pad_ref[n_ranges + e], zero, 0)

    def body(r, c):
        for k in range(TOP_K):
            src_ref[pos_ref[k * t + r]] = i * t + r
        return c
    lax.fori_loop(0, t, body, 0, unroll=8)


def _row_tokens(pos_tiles, pad_ranges, rows):
    t = INVERT_TILE
    return pl.pallas_call(
        _row_tokens_kernel,
        grid=(pos_tiles.shape[0] // (TOP_K * t),),
        in_specs=[pl.BlockSpec((TOP_K * t,), lambda i: (i,), memory_space=pltpu.SMEM),
                  pl.BlockSpec(memory_space=pltpu.SMEM)],
        out_specs=pl.BlockSpec(memory_space=pltpu.SMEM),
        out_shape=jax.ShapeDtypeStruct((rows,), jnp.int32),
        compiler_params=_params(("arbitrary",)),
        name="row_tokens",
    )(pos_tiles, pad_ranges)


def _pos_tiles(pos1, pos2, tile):
    nt = pos1.shape[0] // tile
    return jnp.stack([pos1.reshape(nt, tile), pos2.reshape(nt, tile)], axis=1).reshape(-1)


def _moe_layer(h, norm_g, layer, router, w1, w3, w2, moe_idx, norm_final, final_norm):
    n = h.shape[0]
    info, info_t, cnt = _route(h, norm_g, router, layer)
    counts = cnt[0, :N_EXPERTS].astype(jnp.int32)
    padded = ((counts + ROW_TILE - 1) // ROW_TILE) * ROW_TILE
    ends = jnp.cumsum(padded)
    starts = ends - padded
    e1 = info_t[0].astype(jnp.int32)
    e2 = info_t[1].astype(jnp.int32)
    pos1 = starts[e1] + info_t[4].astype(jnp.int32)
    pos2 = starts[e2] + info_t[5].astype(jnp.int32)

    n_tiles = (n * TOP_K) // ROW_TILE + N_EXPERTS
    tile_start = jnp.arange(n_tiles, dtype=jnp.int32) * ROW_TILE
    in_use = tile_start < ends[-1]
    tile_expert = jnp.sum((tile_start[:, None] >= ends[None, :]).astype(jnp.int32), axis=1)
    last_expert = jnp.sum((ends[-1] - 1 >= ends).astype(jnp.int32))
    tile_expert = jnp.where(in_use, tile_expert, last_expert).astype(jnp.int32)
    tile_rows = jnp.clip((starts + counts)[tile_expert] - tile_start, 0, ROW_TILE)
    tile_valid = jnp.where(in_use, (tile_rows + SUB_ROWS - 1) // SUB_ROWS, 0).astype(jnp.int32)

    n_rows = n_tiles * ROW_TILE
    pad_ranges = jnp.concatenate([starts + counts, ends[-1:], ends,
                                  jnp.full((1,), n_rows, jnp.int32)]).astype(jnp.int32)
    src = _row_tokens(_pos_tiles(pos1, pos2, INVERT_TILE), pad_ranges, n_rows)

    ys = _ffn_experts(h, norm_g, layer, w1, w3, w2, moe_idx, tile_expert, tile_valid, src)
    return _combine(h, info, norm_final, ys, _pos_tiles(pos1, pos2, COMBINE_TILE), final_norm)


def kernel(x, positions, norm_mix, norm_ffn, norm_final, w_in, conv_w, w_ret_out, w_conv_out,
           w_out, ffn_w1, ffn_w3, ffn_w2, router, moe_w1, moe_w3, moe_w2):
    bsz, seq, _ = x.shape
    depth = w_in.shape[0]
    n = bsz * seq
    h = x.reshape(n, D_MODEL)
    norm_mix = norm_mix.reshape(depth, 1, D_MODEL)
    norm_ffn = norm_ffn.reshape(depth, 1, D_MODEL)
    cos, sin = _rope_tables(positions)
    consts = _retention_consts()
    w_ret_out, w_conv_out, w_out = (w.astype(BF16) for w in (w_ret_out, w_conv_out, w_out))
    for layer in range(depth):
        proj = _inproj(_norm_rows(h, norm_mix, layer, BF16), w_in, layer)
        og = _retention(proj, cos, sin, consts, bsz, seq)
        h = _mix_out(og, proj, h, conv_w, w_ret_out, w_conv_out, w_out, layer, seq)
        i = layer // 2
        last = layer == depth - 1
        if layer % 2 == 0:
            h = _ffn_dense(h, norm_ffn, layer, ffn_w1, ffn_w3, ffn_w2, i)
            if last:
                h = _norm_rows(h, norm_final.reshape(1, 1, D_MODEL), 0, F32)
        else:
            h = _moe_layer(h, norm_ffn, layer, router[i], moe_w1, moe_w3, moe_w2, i,
                           norm_final, last)
    return h.reshape(bsz, seq, D_MODEL)
```
